```python
import jax, jax.numpy as jnp
from jax import lax
import numpy as np

D_MODEL = 1024
BATCH = 8
SEQ = 8192
DEPTH = 4

CHUNK = 64
N_META = 16
HEAD_DIM = 64
D_RW = D_MODEL // 2
H_RW = D_RW // HEAD_DIM
D_FX = D_MODEL // 2
H_FX = D_FX // HEAD_DIM
LORA_W = 64
LORA_A = 64
LORA_G = 128
D_FF = 4 * D_MODEL
QBLK = 128
C_RW = 3 * D_RW + LORA_W + LORA_A + LORA_G
C_FX = 3 * D_FX + H_FX
C_GATE = 2 * D_MODEL
C_IN = C_RW + C_FX + C_GATE
NORM_EPS = 1e-6
GN_EPS = 64e-5
DECAY_OFFSET = 0.5

kernel_name = "rwkv7_fox_gated_hybrid_trunk"


def rmsnorm(x, g):
    xf = x.astype(jnp.float32)
    y = xf * lax.rsqrt(jnp.mean(xf * xf, axis=-1, keepdims=True) + NORM_EPS)
    return (y * g.astype(jnp.float32)).astype(x.dtype)


def token_shift(f):
    return jnp.pad(f, ((0, 0), (1, 0), (0, 0)))[:, :-1]


def wkv7_scan(r, w, k, v, a, b):
    Bn, Ln, Hn, Nn = r.shape
    s0 = jnp.zeros((Bn, Hn, Nn, Nn), jnp.float32)
    xs = tuple(jnp.moveaxis(t, 1, 0) for t in (r, w, k, v, a, b))

    def step(S, inp):
        rt, wt, kt, vt, at, bt = inp
        sa = jnp.einsum('bhij,bhj->bhi', S, at)
        S = S * wt[:, :, None, :] + sa[..., None] * bt[:, :, None, :] + vt[..., None] * kt[:, :, None, :]
        yt = jnp.einsum('bhij,bhj->bhi', S, rt)
        return S, yt

    _, y = lax.scan(step, s0, xs)
    return jnp.moveaxis(y, 0, 1)


def rwkv7_branch(feat, mu, w0, w_lora_up, a0, a_lora_up, g_lora_up, k_k, k_a, r_k, lnx_g, lnx_b):
    Bn, Ln, _ = feat.shape
    f = feat.astype(jnp.float32)
    f = f + mu * (token_shift(f) - f)
    r, k, v, wd, ad, gd = jnp.split(
        f, [D_RW, 2 * D_RW, 3 * D_RW, 3 * D_RW + LORA_W, 3 * D_RW + LORA_W + LORA_A], axis=-1)
    w_log = -jax.nn.softplus(-(w0 + jnp.tanh(wd) @ w_lora_up)) - DECAY_OFFSET
    decay = jnp.exp(-jnp.exp(w_log))
    a = jax.nn.sigmoid(a0 + ad @ a_lora_up)
    g = jax.nn.sigmoid(gd) @ g_lora_up
    hv = lambda t: t.reshape(Bn, Ln, H_RW, HEAD_DIM)
    kk = hv(k * k_k)
    kk = kk / jnp.maximum(jnp.sqrt(jnp.sum(kk * kk, axis=-1, keepdims=True)), 1e-12)
    k = k * (1.0 + (a - 1.0) * k_a)
    a_h = hv(a)
    y = wkv7_scan(hv(r), hv(decay), hv(k), hv(v), -kk, kk * a_h)
    mean = jnp.mean(y, axis=-1, keepdims=True)
    var = jnp.mean(jnp.square(y - mean), axis=-1, keepdims=True)
    y = (y - mean) * lax.rsqrt(var + GN_EPS)
    y = y.reshape(Bn, Ln, D_RW) * lnx_g + lnx_b
    bonus = jnp.sum(hv(r) * hv(k) * r_k, axis=-1, keepdims=True) * hv(v)
    return (y + bonus.reshape(Bn, Ln, D_RW)) * g


def head_rmsnorm(t, g):
    tf = t.astype(jnp.float32)
    return tf * lax.rsqrt(jnp.mean(tf * tf, axis=-1, keepdims=True) + NORM_EPS) * g


def fox_branch(feat, b_f, q_gain, k_gain):
    Bn, Ln, _ = feat.shape
    q, k, v, fl = jnp.split(feat, [D_FX, 2 * D_FX, 3 * D_FX], axis=-1)
    q = head_rmsnorm(q.reshape(Bn, Ln, H_FX, HEAD_DIM), q_gain)
    k = head_rmsnorm(k.reshape(Bn, Ln, H_FX, HEAD_DIM), k_gain)
    v = v.reshape(Bn, Ln, H_FX, HEAD_DIM)
    log_f = jax.nn.log_sigmoid(fl.astype(jnp.float32) + b_f)
    Lp = -(-Ln // QBLK) * QBLK
    pad = Lp - Ln
    q = jnp.pad(q, ((0, 0), (0, pad), (0, 0), (0, 0)))
    k = jnp.pad(k, ((0, 0), (0, pad), (0, 0), (0, 0)))
    v = jnp.pad(v, ((0, 0), (0, pad), (0, 0), (0, 0)))
    F = jnp.cumsum(jnp.pad(log_f, ((0, 0), (0, pad), (0, 0))), axis=1)
    F = jnp.transpose(F, (0, 2, 1))
    kpos = jnp.arange(Lp)
    scale = HEAD_DIM ** -0.5

    def block(i):
        start = i * QBLK
        qb = lax.dynamic_slice_in_dim(q, start, QBLK, axis=1)
        Fq = lax.dynamic_slice_in_dim(F, start, QBLK, axis=2)
        logits = jnp.einsum('bqhd,bkhd->bhqk', qb, k).astype(jnp.float32) * scale
        logits = logits + (Fq[..., None] - F[:, :, None, :])
        qpos = start + jnp.arange(QBLK)
        mask = kpos[None, :] <= qpos[:, None]
        p = jax.nn.softmax(jnp.where(mask, logits, -jnp.inf), axis=-1)
        return jnp.einsum('bhqk,bkhd->bqhd', p.astype(v.dtype), v)

    out = lax.map(block, jnp.arange(Lp // QBLK))
    out = jnp.moveaxis(out, 0, 1).reshape(Bn, Lp, D_FX)
    return out[:, :Ln]


def setup_inputs(seed: int = 0) -> dict:
    key = jax.random.key(seed)
    ks = jax.random.split(key, 26)
    nrm = lambda k, shape, s: jax.random.normal(k, shape, jnp.float32) * s
    L = DEPTH
    return {
        "x": nrm(ks[0], (BATCH, SEQ, D_MODEL), 1.0),
        "meta": nrm(ks[1], (N_META, D_MODEL), 1.0),
        "norm_mix": 1.0 + nrm(ks[2], (L, D_MODEL), 0.02),
        "w_in": nrm(ks[3], (L, D_MODEL, C_IN), D_MODEL ** -0.5),
        "b_gate": nrm(ks[4], (L, C_GATE), 0.1),
        "b_f": 1.0 + nrm(ks[5], (L, H_FX), 0.5),
        "tm_mu": jax.random.uniform(ks[6], (L, C_RW), jnp.float32),
        "w0": jax.random.uniform(ks[7], (L, D_RW), jnp.float32, -4.0, 1.0),
        "w_lora_up": nrm(ks[8], (L, LORA_W, D_RW), LORA_W ** -0.5),
        "a0": nrm(ks[9], (L, D_RW), 0.1),
        "a_lora_up": nrm(ks[10], (L, LORA_A, D_RW), LORA_A ** -0.5),
        "g_lora_up": nrm(ks[11], (L, LORA_G, D_RW), LORA_G ** -0.5),
        "k_k": 0.85 + nrm(ks[12], (L, D_RW), 0.05),
        "k_a": 1.0 + nrm(ks[13], (L, D_RW), 0.05),
        "r_k": nrm(ks[14], (L, H_RW, HEAD_DIM), 0.1),
        "lnx_g": 1.0 + nrm(ks[15], (L, D_RW), 0.02),
        "lnx_b": nrm(ks[16], (L, D_RW), 0.02),
        "q_gain": 1.0 + nrm(ks[17], (L, HEAD_DIM), 0.02),
        "k_gain": 1.0 + nrm(ks[18], (L, HEAD_DIM), 0.02),
        "w_out_rw": nrm(ks[19], (L, D_RW, D_MODEL), D_RW ** -0.5),
        "w_out_fx": nrm(ks[20], (L, D_FX, D_MODEL), D_FX ** -0.5),
        "w_o": nrm(ks[21], (L, D_MODEL, D_MODEL), D_MODEL ** -0.5),
        "norm_mlp": 1.0 + nrm(ks[22], (L, D_MODEL), 0.02),
        "w_up": nrm(ks[23], (L, D_MODEL, D_FF), D_MODEL ** -0.5),
        "w_down": nrm(ks[24], (L, D_FF, D_MODEL), D_FF ** -0.5),
    }


def reference(x, meta, norm_mix, w_in, b_gate, b_f, tm_mu, w0, w_lora_up, a0, a_lora_up,
              g_lora_up, k_k, k_a, r_k, lnx_g, lnx_b, q_gain, k_gain, w_out_rw, w_out_fx,
              w_o, norm_mlp, w_up, w_down):
    Bn = x.shape[0]
    meta_b = jnp.broadcast_to(meta.astype(x.dtype)[None], (Bn, N_META, D_MODEL))
    h = jnp.concatenate([meta_b, x], axis=1)
    for l in range(DEPTH):
        u = rmsnorm(h, norm_mix[l])
        proj = u @ w_in[l]
        feat_rw = proj[..., :C_RW]
        feat_fx = proj[..., C_RW:C_RW + C_FX]
        gates = jax.nn.sigmoid((proj[..., C_RW + C_FX:] + b_gate[l]).astype(jnp.float32))
        y_rw = rwkv7_branch(feat_rw, tm_mu[l], w0[l], w_lora_up[l], a0[l], a_lora_up[l],
                            g_lora_up[l], k_k[l], k_a[l], r_k[l], lnx_g[l], lnx_b[l])
        y_fx = fox_branch(feat_fx, b_f[l], q_gain[l], k_gain[l])
        merged = gates[..., :D_MODEL] * (y_rw @ w_out_rw[l]) + gates[..., D_MODEL:] * (y_fx @ w_out_fx[l])
        h = h + (merged @ w_o[l]).astype(h.dtype)
        z = rmsnorm(h, norm_mlp[l])
        h = h + (jnp.square(jax.nn.relu(z @ w_up[l])) @ w_down[l]).astype(h.dtype)
    return h[:, N_META:]
```

```python
import functools
import math

import jax
import jax.numpy as jnp
from jax import lax
from jax.experimental import pallas as pl
from jax.experimental.pallas import tpu as pltpu

F32 = jnp.float32
BF16 = jnp.bfloat16

N_META = 16
HEAD_DIM = 64
LORA_W = 64
LORA_A = 64
LORA_G = 128
NORM_EPS = 1e-6
GN_EPS = 64e-5
DECAY_OFFSET = 0.5
LOG2E = 1.4426950408889634

LANES = 128
SEQ_ALIGN = 128
WKV_CHUNK = 128
VMEM_LIMIT_BYTES = 56 * 1024 * 1024
NEG_BIG = -1e30


def _dot(a, b):
    return jnp.dot(a, b, preferred_element_type=F32)


def _dot_nt(a, b):
    return lax.dot_general(a, b, (((1,), (1,)), ((), ())), preferred_element_type=F32)


def _split2(x):
    hi = x.astype(BF16)
    lo = (x - hi.astype(F32)).astype(BF16)
    return hi, lo


def _split3(x):
    hi = x.astype(BF16)
    r1 = x - hi.astype(F32)
    mid = r1.astype(BF16)
    lo = (r1 - mid.astype(F32)).astype(BF16)
    return hi, mid, lo


def _dot_exact_rhs(x, m_bf16):
    hi, mid, lo = _split3(x)
    return _dot(hi, m_bf16) + _dot(mid, m_bf16) + _dot(lo, m_bf16)


def _dot_exact_lhs(m_bf16, x):
    hi, mid, lo = _split3(x)
    return _dot(m_bf16, hi) + _dot(m_bf16, mid) + _dot(m_bf16, lo)


def _sigmoid(x):
    return 1.0 / (1.0 + jnp.exp(-x))


def _softplus(x):
    return jnp.maximum(x, 0.0) + jnp.log(1.0 + jnp.exp(-jnp.abs(x)))


def _const_spec(shape):
    nd = len(shape)
    return pl.BlockSpec(shape, lambda *_: (0,) * nd)


def _params(sem):
    return pltpu.CompilerParams(dimension_semantics=sem, vmem_limit_bytes=VMEM_LIMIT_BYTES)


def _inproj_kernel(h_ref, g_ref, wrw_ref, wfx_ref, wg_ref, bg_ref, rw_ref, fx_ref, gate_ref):
    x = h_ref[...]
    ms = jnp.mean(x * x, axis=-1, keepdims=True)
    u = (x * lax.rsqrt(ms + NORM_EPS) * g_ref[...]).astype(BF16)
    rw_ref[...] = _dot(u, wrw_ref[...])
    fx_ref[...] = _dot(u, wfx_ref[...])
    gate_ref[...] = _sigmoid(_dot(u, wg_ref[...]) + bg_ref[...]).astype(BF16)


def _inproj(h2, g, wrw, wfx, wg, bg, tm):
    rows, d = h2.shape
    n_rw, n_fx, n_g = wrw.shape[1], wfx.shape[1], wg.shape[1]
    return pl.pallas_call(
        _inproj_kernel,
        grid=(rows // tm,),
        in_specs=[
            pl.BlockSpec((tm, d), lambda i: (i, 0)),
            _const_spec((1, d)),
            _const_spec((d, n_rw)),
            _const_spec((d, n_fx)),
            _const_spec((d, n_g)),
            _const_spec((1, n_g)),
        ],
        out_specs=[
            pl.BlockSpec((tm, n_rw), lambda i: (i, 0)),
            pl.BlockSpec((tm, n_fx), lambda i: (i, 0)),
            pl.BlockSpec((tm, n_g), lambda i: (i, 0)),
        ],
        out_shape=[
            jax.ShapeDtypeStruct((rows, n_rw), F32),
            jax.ShapeDtypeStruct((rows, n_fx), F32),
            jax.ShapeDtypeStruct((rows, n_g), BF16),
        ],
        compiler_params=_params(("parallel",)),
        name="inproj",
    )(h2, g, wrw, wfx, wg, bg)


def _wkv_kernel(x_ref, mu_ref, w0_ref, wlw_ref, a0_ref, wla_ref, wlg_ref, kk_ref, ka_ref, rk_ref,
                lng_ref, lnb_ref, bd_ref, o_ref, m_ref, prev_ref, *, d_rw):
    T = WKV_CHUNK
    c = pl.program_id(1)

    @pl.when(c == 0)
    def _():
        m_ref[...] = jnp.zeros_like(m_ref)
        prev_ref[...] = jnp.zeros_like(prev_ref)

    x = x_ref[...]
    row = lax.broadcasted_iota(jnp.int32, x.shape, 0)
    shifted = jnp.where(row == 0, prev_ref[...], pltpu.roll(x, 1, 0))
    prev_ref[...] = x[T - 1:T, :]
    f = x + mu_ref[...] * (shifted - x)

    r = f[:, 0:d_rw]
    k = f[:, d_rw:2 * d_rw]
    v = f[:, 2 * d_rw:3 * d_rw]
    wa_d = f[:, 3 * d_rw:3 * d_rw + LORA_W + LORA_A]
    g_d = f[:, 3 * d_rw + LORA_W + LORA_A:]

    bd = bd_ref[...]

    def segsum(t):
        hi, lo = _split2(t)
        return _dot(hi, bd) + _dot(lo, bd)

    w_log = -_softplus(-(w0_ref[...] + _dot(jnp.tanh(wa_d).astype(BF16), wlw_ref[...]))) - DECAY_OFFSET
    lw = -jnp.exp(w_log)
    a_lr = _sigmoid(a0_ref[...] + _dot(wa_d.astype(BF16), wla_ref[...]))
    g = _dot(_sigmoid(g_d).astype(BF16), wlg_ref[...])
    kk = k * kk_ref[...]
    kk = kk / jnp.maximum(jnp.sqrt(segsum(kk * kk)), 1e-12)
    k2 = k * (1.0 + (a_lr - 1.0) * ka_ref[...])
    a_vec = -kk
    b_vec = kk * a_lr

    ri = lax.broadcasted_iota(jnp.int32, (T, T), 0)
    ci = lax.broadcasted_iota(jnp.int32, (T, T), 1)
    strict = ci < ri
    incl = ci <= ri
    eye = ci == ri
    tri_incl_bf = incl.astype(BF16)
    rc_xor = jnp.bitwise_xor(ri, ci)
    level_masks = []
    n = 1
    while n < T:
        level_masks.append(strict & (rc_xor >= n) & (rc_xor < 2 * n))
        n *= 2
    lane = lax.broadcasted_iota(jnp.int32, (T, LANES), 1)
    head0 = lane < HEAD_DIM
    pr = lax.broadcasted_iota(jnp.int32, (LANES, LANES), 0)
    pc = lax.broadcasted_iota(jnp.int32, (LANES, LANES), 1)
    same_head = (pr < HEAD_DIM) == (pc < HEAD_DIM)
    eye_p = pr == pc

    ys = []
    for p in range(d_rw // LANES):
        sl = slice(p * LANES, (p + 1) * LANES)
        rp, lwp, kp, vp, ap, bp = r[:, sl], lw[:, sl], k2[:, sl], v[:, sl], a_vec[:, sl], b_vec[:, sl]
        cl = _dot_exact_lhs(tri_incl_bf, lwp)
        cl_last = cl[T - 1:T, :]
        dfwd = jnp.exp(cl)
        dinv = jnp.exp(-cl)
        dend = jnp.exp(cl_last - cl)
        at = ap * jnp.exp(cl - lwp)
        bt = bp * dinv
        kt = kp * dinv
        rt = rp * dfwd
        bdend = (bp * dend).astype(BF16)
        kdend = (kp * dend).astype(BF16)
        vb = vp.astype(BF16)
        atb = at.astype(BF16)

        zero = jnp.zeros_like(at)
        lhs = jnp.concatenate([jnp.where(head0, at, zero), jnp.where(head0, zero, at),
                               jnp.where(head0, rt, zero), jnp.where(head0, zero, rt)], axis=0).astype(BF16)
        rhs = jnp.concatenate([bt, kt], axis=0).astype(BF16)
        sc = _dot_nt(lhs, rhs)

        ap_h, w2_h, qp_h, yl_h = [], [], [], []
        for hh in range(2):
            aab = jnp.where(strict, sc[hh * T:(hh + 1) * T, 0:T], 0.0)
            aak = jnp.where(strict, sc[hh * T:(hh + 1) * T, T:2 * T], 0.0).astype(BF16)
            arb = jnp.where(incl, sc[(2 + hh) * T:(3 + hh) * T, 0:T], 0.0).astype(BF16)
            ark = jnp.where(incl, sc[(2 + hh) * T:(3 + hh) * T, T:2 * T], 0.0).astype(BF16)
            tinv = jnp.where(eye, 1.0, jnp.where(level_masks[0], aab, 0.0)).astype(BF16)
            for lm in level_masks[1:]:
                ao = jnp.where(lm, aab, 0.0).astype(BF16)
                tinv = (tinv.astype(F32) + _dot(_dot(tinv, ao).astype(BF16), tinv)).astype(BF16)
            ap_x = _dot(tinv, atb)
            w2_x = _dot(tinv, _dot(aak, vb).astype(BF16))
            ap_h.append(ap_x)
            w2_h.append(w2_x)
            yl_h.append((arb, ark))
        apn = jnp.where(head0, ap_h[0], ap_h[1]).astype(BF16)
        w2 = jnp.where(head0, w2_h[0], w2_h[1]).astype(BF16)
        for hh in range(2):
            arb, ark = yl_h[hh]
            qp_h.append(_dot(arb, apn))
            yl_h[hh] = _dot(arb, w2) + _dot(ark, vb)
        qp = rt + jnp.where(head0, qp_h[0], qp_h[1])
        yl = jnp.where(head0, yl_h[0], yl_h[1])

        bdend_t = bdend.astype(F32).T.astype(BF16)
        kdend_t = kdend.astype(F32).T.astype(BF16)
        g_lr = _dot(bdend_t, apn)
        dt_row = jnp.exp(cl_last)
        g_mat = jnp.where(same_head, g_lr, 0.0) + jnp.where(eye_p, dt_row, 0.0)
        c_mat = jnp.where(same_head, _dot(bdend_t, w2) + _dot(kdend_t, vb), 0.0)

        m0 = m_ref[p].astype(BF16)
        ys.append(_dot(qp.astype(BF16), m0) + yl)
        m_ref[p] = _dot(g_mat.astype(BF16), m0) + c_mat

    y = jnp.concatenate(ys, axis=1)
    inv_n = 1.0 / HEAD_DIM
    mean = segsum(y) * inv_n
    yc = y - mean
    var = segsum(yc * yc) * inv_n
    yn = yc * lax.rsqrt(var + GN_EPS) * lng_ref[...] + lnb_ref[...]
    bonus = segsum(r * k2 * rk_ref[...]) * v
    o_ref[...] = ((yn + bonus) * g).astype(BF16)


def _wkv(feat_rw, mu, w0, wlw, a0, wla, wlg, k_k, k_a, r_k, lnx_g, lnx_b, bd, d_rw):
    B, Lp, c_rw = feat_rw.shape
    T = WKV_CHUNK
    vec = _const_spec((1, d_rw))
    return pl.pallas_call(
        functools.partial(_wkv_kernel, d_rw=d_rw),
        grid=(B, Lp // T),
        in_specs=[
            pl.BlockSpec((None, T, c_rw), lambda b, c: (b, c, 0)),
            _const_spec((1, c_rw)),
            vec, _const_spec((LANES, d_rw)), vec, _const_spec((LANES, d_rw)), _const_spec((LORA_G, d_rw)),
            vec, vec, vec, vec, vec,
            _const_spec((d_rw, d_rw)),
        ],
        out_specs=pl.BlockSpec((None, T, d_rw), lambda b, c: (b, c, 0)),
        out_shape=jax.ShapeDtypeStruct((B, Lp, d_rw), BF16),
        scratch_shapes=[pltpu.VMEM((d_rw // LANES, LANES, LANES), F32),
                        pltpu.VMEM((1, c_rw), F32)],
        compiler_params=_params(("parallel", "arbitrary")),
        name="wkv",
    )(feat_rw, mu, w0, wlw, a0, wla, wlg, k_k, k_a, r_k, lnx_g, lnx_b, bd)


def _fxprep_kernel(fx_ref, qg_ref, kg_ref, bf_ref, sq_ref, sk_ref, cq_ref, ck_ref, cv_ref,
                   q_ref, kt_ref, v_ref, carry_ref, *, n_heads):
    i = pl.program_id(1)

    @pl.when(i == 0)
    def _():
        carry_ref[...] = jnp.zeros_like(carry_ref)

    hw = n_heads * LANES
    x = fx_ref[...]
    tm = x.shape[0]
    fl = x[:, 3 * hw:3 * hw + LANES]
    z = fl + bf_ref[...]
    log_f = jnp.minimum(z, 0.0) - jnp.log(1.0 + jnp.exp(-jnp.abs(z)))
    lane = lax.broadcasted_iota(jnp.int32, (tm, LANES), 1)
    log_f = jnp.where(lane < n_heads, log_f, 0.0)
    ri = lax.broadcasted_iota(jnp.int32, (tm, tm), 0)
    ci = lax.broadcasted_iota(jnp.int32, (tm, tm), 1)
    tri = (ci <= ri).astype(BF16)
    fcum = carry_ref[...] + _dot_exact_lhs(tri, log_f)
    carry_ref[...] = fcum[tm - 1:tm, :]
    hi, mid, lo = _split3(fcum * LOG2E)
    parts = jnp.concatenate([hi, mid, lo], axis=1)
    extra_q = _dot(parts, sq_ref[...]) + cq_ref[...]
    extra_k = _dot(parts, sk_ref[...]) + ck_ref[...]

    inv_n = 1.0 / HEAD_DIM
    for h in range(n_heads):
        sl = slice(h * LANES, (h + 1) * LANES)
        qh = x[:, sl]
        kh = x[:, hw + h * LANES: hw + (h + 1) * LANES]
        qn = qh * lax.rsqrt(jnp.sum(qh * qh, axis=-1, keepdims=True) * inv_n + NORM_EPS) * qg_ref[:, sl]
        kn = kh * lax.rsqrt(jnp.sum(kh * kh, axis=-1, keepdims=True) * inv_n + NORM_EPS) * kg_ref[:, sl]
        q_ref[:, sl] = (qn + extra_q[:, sl]).astype(BF16)
        kt_ref[sl, :] = (kn + extra_k[:, sl]).T.astype(BF16)
    v_ref[...] = (x[:, 2 * hw:3 * hw] + cv_ref[...]).astype(BF16)


def _fxprep(feat_fx, qg, kg, bf, sq, sk, cq, ck, cv, n_heads, tm):
    B, Lp, n_fx = feat_fx.shape
    hw = n_heads * LANES
    return pl.pallas_call(
        functools.partial(_fxprep_kernel, n_heads=n_heads),
        grid=(B, Lp // tm),
        in_specs=[
            pl.BlockSpec((None, tm, n_fx), lambda b, i: (b, i, 0)),
            _const_spec((1, hw)), _const_spec((1, hw)), _const_spec((1, LANES)),
            _const_spec((3 * LANES, hw)), _const_spec((3 * LANES, hw)),
            _const_spec((1, hw)), _const_spec((1, hw)), _const_spec((1, hw)),
        ],
        out_specs=[
            pl.BlockSpec((None, tm, hw), lambda b, i: (b, i, 0)),
            pl.BlockSpec((None, hw, tm), lambda b, i: (b, 0, i)),
            pl.BlockSpec((None, tm, hw), lambda b, i: (b, i, 0)),
        ],
        out_shape=[
            jax.ShapeDtypeStruct((B, Lp, hw), BF16),
            jax.ShapeDtypeStruct((B, hw, Lp), BF16),
            jax.ShapeDtypeStruct((B, Lp, hw), BF16),
        ],
        scratch_shapes=[pltpu.VMEM((1, LANES), F32)],
        compiler_params=_params(("parallel", "arbitrary")),
        name="fxprep",
    )(feat_fx, qg, kg, bf, sq, sk, cq, ck, cv)


def _attn_kernel(q_ref, kt_ref, v_ref, o_ref, *, blk):
    i = pl.program_id(2)
    ri = lax.broadcasted_iota(jnp.int32, (blk, blk), 0)
    ci = lax.broadcasted_iota(jnp.int32, (blk, blk), 1)
    causal = ci <= ri
    lane = lax.broadcasted_iota(jnp.int32, (blk, LANES), 1)
    outs = []
    for hh in range(2):
        hs = slice(hh * LANES, (hh + 1) * LANES)
        q = q_ref[:, hs]

        def step(j, carry, masked):
            m, acc = carry
            off = pl.multiple_of(j * blk, blk)
            s = _dot(q, kt_ref[hs, pl.ds(off, blk)])
            if masked:
                s = jnp.where(causal, s, NEG_BIG)
            m_new = jnp.maximum(m, jnp.max(s, axis=-1, keepdims=True))
            p = jnp.exp2(s - m_new)
            acc = jnp.exp2(m - m_new) * acc + _dot(p.astype(BF16), v_ref[pl.ds(off, blk), hs])
            return m_new, acc

        carry = (jnp.full((blk, 1), NEG_BIG, F32), jnp.zeros((blk, LANES), F32))
        carry = lax.fori_loop(0, i, functools.partial(step, masked=False), carry)
        _, acc = step(i, carry, True)
        den_lane = HEAD_DIM if hh == 0 else 0
        outs.append(acc / acc[:, den_lane:den_lane + 1])
    o_ref[...] = jnp.where(lane < HEAD_DIM, outs[0], outs[1]).astype(BF16)


def _attn(qp, ktp, vp, blk):
    B, Lp, hw = qp.shape
    n_pairs = hw // (2 * LANES)
    return pl.pallas_call(
        functools.partial(_attn_kernel, blk=blk),
        grid=(B, n_pairs, Lp // blk),
        in_specs=[
            pl.BlockSpec((None, blk, 2 * LANES), lambda b, p, i: (b, i, p)),
            pl.BlockSpec((None, 2 * LANES, Lp), lambda b, p, i: (b, p, 0)),
            pl.BlockSpec((None, Lp, 2 * LANES), lambda b, p, i: (b, 0, p)),
        ],
        out_specs=pl.BlockSpec((None, blk, LANES), lambda b, p, i: (b, i, p)),
        out_shape=jax.ShapeDtypeStruct((B, Lp, n_pairs * LANES), BF16),
        compiler_params=_params(("parallel", "parallel", "arbitrary")),
        name="attn",
    )(qp, ktp, vp)


def _outmlp_kernel(h_ref, yrw_ref, yfx_ref, gate_ref, wrw_ref, wfx_ref, wo_ref, gm_ref, wup_ref,
                   wdn_ref, o_ref, *, ff_chunk):
    d = h_ref.shape[1]
    gates = gate_ref[...].astype(F32)
    merged = gates[:, :d] * _dot(yrw_ref[...], wrw_ref[...]) + gates[:, d:] * _dot(yfx_ref[...], wfx_ref[...])
    h1 = h_ref[...] + _dot(merged.astype(BF16), wo_ref[...])
    ms = jnp.mean(h1 * h1, axis=-1, keepdims=True)
    z = (h1 * lax.rsqrt(ms + NORM_EPS) * gm_ref[...]).astype(BF16)
    acc = h1
    for c0 in range(0, wup_ref.shape[1], ff_chunk):
        u = jnp.maximum(_dot(z, wup_ref[:, c0:c0 + ff_chunk]), 0.0)
        acc = acc + _dot((u * u).astype(BF16), wdn_ref[c0:c0 + ff_chunk, :])
    o_ref[...] = acc


def _outmlp(h2, yrw, yfx, gates, wrw, wfx, wo, gm, wup, wdn, tm):
    rows, d = h2.shape
    d_rw, d_fx, d_ff = wrw.shape[0], wfx.shape[0], wup.shape[1]
    return pl.pallas_call(
        functools.partial(_outmlp_kernel, ff_chunk=min(d_ff, 1024)),
        grid=(rows // tm,),
        in_specs=[
            pl.BlockSpec((tm, d), lambda i: (i, 0)),
            pl.BlockSpec((tm, d_rw), lambda i: (i, 0)),
            pl.BlockSpec((tm, d_fx), lambda i: (i, 0)),
            pl.BlockSpec((tm, 2 * d), lambda i: (i, 0)),
            _const_spec((d_rw, d)), _const_spec((d_fx, d)), _const_spec((d, d)),
            _const_spec((1, d)), _const_spec((d, d_ff)), _const_spec((d_ff, d)),
        ],
        out_specs=pl.BlockSpec((tm, d), lambda i: (i, 0)),
        out_shape=jax.ShapeDtypeStruct((rows, d), F32),
        compiler_params=_params(("parallel",)),
        name="outmlp",
    )(h2, yrw, yfx, gates, wrw, wfx, wo, gm, wup, wdn)


def _largest_divisor(n, candidates):
    for c in candidates:
        if n % c == 0:
            return c
    raise ValueError(f"no tile in {candidates} divides {n}")


def _head_pad_cols(w, n_heads, upper_for_odd):
    kdim = w.shape[0]
    w3 = w.reshape(kdim, n_heads, HEAD_DIM)
    z = jnp.zeros_like(w3)
    lower = jnp.concatenate([w3, z], axis=2)
    if not upper_for_odd:
        return lower.reshape(kdim, n_heads * LANES)
    upper = jnp.concatenate([z, w3], axis=2)
    odd = (jnp.arange(n_heads) % 2 == 1)[None, :, None]
    return jnp.where(odd, upper, lower).reshape(kdim, n_heads * LANES)


def _attn_constants(n_heads):
    hw = n_heads * LANES
    sq = jnp.zeros((3 * LANES, hw), F32)
    sk = jnp.zeros((3 * LANES, hw), F32)
    cq = jnp.zeros((1, hw), F32)
    ck = jnp.zeros((1, hw), F32)
    cv = jnp.zeros((1, hw), F32)
    for h in range(n_heads):
        base = h * LANES + HEAD_DIM
        for part in range(3):
            sq = sq.at[part * LANES + h, base + part].set(1.0)
            sk = sk.at[part * LANES + h, base + 3 + part].set(-1.0)
            cq = cq.at[0, base + 3 + part].set(1.0)
            ck = ck.at[0, base + part].set(1.0)
        cv = cv.at[0, h * LANES + (HEAD_DIM if h % 2 == 0 else 0)].set(1.0)
    return sq.astype(BF16), sk.astype(BF16), cq, ck, cv


def kernel(x, meta, norm_mix, w_in, b_gate, b_f, tm_mu, w0, w_lora_up, a0, a_lora_up, g_lora_up, k_k, k_a,
           r_k, lnx_g, lnx_b, q_gain, k_gain, w_out_rw, w_out_fx, w_o, norm_mlp, w_up, w_down):
    B, S, D = x.shape
    depth = w_in.shape[0]
    d_rw = w0.shape[1]
    d_fx = w_out_fx.shape[1]
    h_fx = b_f.shape[1]
    c_rw = 3 * d_rw + LORA_W + LORA_A + LORA_G
    c_fx = 3 * d_fx + h_fx
    assert d_rw % LANES == 0 and h_fx % 2 == 0 and d_fx == h_fx * HEAD_DIM and h_fx <= LANES
    assert LORA_W + LORA_A == LANES and LORA_G == LANES and HEAD_DIM * 2 == LANES

    L = N_META + S
    Lp = -(-L // SEQ_ALIGN) * SEQ_ALIGN
    rows = B * Lp
    tm_row = _largest_divisor(rows, (256, 128))
    tm_prep = _largest_divisor(Lp, (256, 128))
    blk = _largest_divisor(Lp, (640, 512, 384, 256, 128))
    hw = h_fx * LANES

    meta_b = jnp.broadcast_to(meta.astype(x.dtype)[None], (B, N_META, D))
    h = jnp.concatenate([meta_b, x, jnp.zeros((B, Lp - L, D), x.dtype)], axis=1).reshape(rows, D)

    sq, sk, cq, ck, cv = _attn_constants(h_fx)
    seg = jnp.arange(d_rw) // HEAD_DIM
    bd = (seg[:, None] == seg[None, :]).astype(BF16)
    zeros_lora = jnp.zeros((LORA_W, d_rw), F32)
    scale = HEAD_DIM ** -0.5
    row = lambda t: t.reshape(1, -1)

    for l in range(depth):
        wl = w_in[l]
        w_rw = wl[:, :c_rw].astype(BF16)
        fx0 = c_rw
        w_fx = jnp.concatenate([
            _head_pad_cols(wl[:, fx0:fx0 + d_fx], h_fx, False),
            _head_pad_cols(wl[:, fx0 + d_fx:fx0 + 2 * d_fx], h_fx, False),
            _head_pad_cols(wl[:, fx0 + 2 * d_fx:fx0 + 3 * d_fx], h_fx, True),
            jnp.pad(wl[:, fx0 + 3 * d_fx:fx0 + c_fx], ((0, 0), (0, LANES - h_fx))),
        ], axis=1).astype(BF16)
        w_g = wl[:, c_rw + c_fx:].astype(BF16)

        feat_rw, feat_fx, gates = _inproj(h, row(norm_mix[l]), w_rw, w_fx, w_g, row(b_gate[l]), tm_row)

        y_rw = _wkv(
            feat_rw.reshape(B, Lp, c_rw), row(tm_mu[l]), row(w0[l]),
            jnp.concatenate([w_lora_up[l], zeros_lora], axis=0).astype(BF16), row(a0[l]),
            jnp.concatenate([zeros_lora, a_lora_up[l]], axis=0).astype(BF16), g_lora_up[l].astype(BF16),
            row(k_k[l]), row(k_a[l]), row(r_k[l]), row(lnx_g[l]), row(lnx_b[l]), bd, d_rw)

        qg = _head_pad_cols(jnp.tile(q_gain[l] * (scale * LOG2E), h_fx)[None, :], h_fx, False)
        kg = _head_pad_cols(jnp.tile(k_gain[l], h_fx)[None, :], h_fx, False)
        bf = jnp.pad(b_f[l], (0, LANES - h_fx))[None, :]
        qp, ktp, vp = _fxprep(feat_fx.reshape(B, Lp, -1), qg, kg, bf, sq, sk, cq, ck, cv, h_fx, tm_prep)
        y_fx = _attn(qp, ktp, vp, blk)

        h = _outmlp(h, y_rw.reshape(rows, d_rw), y_fx.reshape(rows, d_fx), gates,
                    w_out_rw[l].astype(BF16), w_out_fx[l].astype(BF16), w_o[l].astype(BF16),
                    row(norm_mlp[l]), w_up[l].astype(BF16), w_down[l].astype(BF16), tm_row)

    return h.reshape(B, Lp, D)[:, N_META:N_META + S]
```

```python
import functools
import math

import jax
import jax.numpy as jnp
from jax import lax
from jax.experimental import pallas as pl
from jax.experimental.pallas import tpu as pltpu

F32 = jnp.float32
BF16 = jnp.bfloat16

N_META = 16
HEAD_DIM = 64
LORA_W = 64
LORA_A = 64
LORA_G = 128
NORM_EPS = 1e-6
GN_EPS = 64e-5
DECAY_OFFSET = 0.5
LOG2E = 1.4426950408889634

LANES = 128
SEQ_ALIGN = 256
SKIP_LOG2 = 160.0
WKV_CHUNK = 128
VMEM_LIMIT_BYTES = 56 * 1024 * 1024
NEG_BIG = -1e30


def _dot(a, b):
    return jnp.dot(a, b, preferred_element_type=F32)


def _dot_nt(a, b):
    return lax.dot_general(a, b, (((1,), (1,)), ((), ())), preferred_element_type=F32)


def _split2(x):
    hi = x.astype(BF16)
    lo = (x - hi.astype(F32)).astype(BF16)
    return hi, lo


def _split3(x):
    hi = x.astype(BF16)
    r1 = x - hi.astype(F32)
    mid = r1.astype(BF16)
    lo = (r1 - mid.astype(F32)).astype(BF16)
    return hi, mid, lo


def _dot_exact_rhs(x, m_bf16):
    hi, mid, lo = _split3(x)
    return _dot(hi, m_bf16) + _dot(mid, m_bf16) + _dot(lo, m_bf16)


def _dot_exact_lhs(m_bf16, x):
    hi, mid, lo = _split3(x)
    return _dot(m_bf16, hi) + _dot(m_bf16, mid) + _dot(m_bf16, lo)


def _sigmoid(x):
    return 1.0 / (1.0 + jnp.exp(-x))


def _softplus(x):
    return jnp.maximum(x, 0.0) + jnp.log(1.0 + jnp.exp(-jnp.abs(x)))


def _const_spec(shape):
    nd = len(shape)
    return pl.BlockSpec(shape, lambda *_: (0,) * nd)


def _params(sem):
    return pltpu.CompilerParams(dimension_semantics=sem, vmem_limit_bytes=VMEM_LIMIT_BYTES)


def _inproj_kernel(h_ref, g_ref, wrw_ref, wfx_ref, wg_ref, bg_ref, rw_ref, fx_ref, gate_ref):
    x = h_ref[...]
    ms = jnp.mean(x * x, axis=-1, keepdims=True)
    u = (x * lax.rsqrt(ms + NORM_EPS) * g_ref[...]).astype(BF16)
    rw_ref[...] = _dot(u, wrw_ref[...])
    fx_ref[...] = _dot(u, wfx_ref[...])
    gate_ref[...] = _sigmoid(_dot(u, wg_ref[...]) + bg_ref[...]).astype(BF16)


def _inproj(h2, g, wrw, wfx, wg, bg, tm):
    rows, d = h2.shape
    n_rw, n_fx, n_g = wrw.shape[1], wfx.shape[1], wg.shape[1]
    return pl.pallas_call(
        _inproj_kernel,
        grid=(rows // tm,),
        in_specs=[
            pl.BlockSpec((tm, d), lambda i: (i, 0)),
            _const_spec((1, d)),
            _const_spec((d, n_rw)),
            _const_spec((d, n_fx)),
            _const_spec((d, n_g)),
            _const_spec((1, n_g)),
        ],
        out_specs=[
            pl.BlockSpec((tm, n_rw), lambda i: (i, 0)),
            pl.BlockSpec((tm, n_fx), lambda i: (i, 0)),
            pl.BlockSpec((tm, n_g), lambda i: (i, 0)),
        ],
        out_shape=[
            jax.ShapeDtypeStruct((rows, n_rw), F32),
            jax.ShapeDtypeStruct((rows, n_fx), F32),
            jax.ShapeDtypeStruct((rows, n_g), BF16),
        ],
        compiler_params=_params(("parallel",)),
        name="inproj",
    )(h2, g, wrw, wfx, wg, bg)


def _wkv_kernel(x_ref, mu_ref, w0_ref, wlw_ref, a0_ref, wla_ref, wlg_ref, kk_ref, ka_ref, rk_ref,
                lng_ref, lnb_ref, bd_ref, o_ref, m_ref, prev_ref, *, d_rw):
    T = WKV_CHUNK
    c = pl.program_id(1)

    @pl.when(c == 0)
    def _():
        m_ref[...] = jnp.zeros_like(m_ref)
        prev_ref[...] = jnp.zeros_like(prev_ref)

    x = x_ref[...]
    row = lax.broadcasted_iota(jnp.int32, x.shape, 0)
    shifted = jnp.where(row == 0, prev_ref[...], pltpu.roll(x, 1, 0))
    prev_ref[...] = x[T - 1:T, :]
    f = x + mu_ref[...] * (shifted - x)

    r = f[:, 0:d_rw]
    k = f[:, d_rw:2 * d_rw]
    v = f[:, 2 * d_rw:3 * d_rw]
    wa_d = f[:, 3 * d_rw:3 * d_rw + LORA_W + LORA_A]
    g_d = f[:, 3 * d_rw + LORA_W + LORA_A:]

    bd = bd_ref[...]

    def segsum(t):
        hi, lo = _split2(t)
        return _dot(hi, bd) + _dot(lo, bd)

    w_log = -_softplus(-(w0_ref[...] + _dot(jnp.tanh(wa_d).astype(BF16), wlw_ref[...]))) - DECAY_OFFSET
    lw = -jnp.exp(w_log)
    a_lr = _sigmoid(a0_ref[...] + _dot(wa_d.astype(BF16), wla_ref[...]))
    g = _dot(_sigmoid(g_d).astype(BF16), wlg_ref[...])
    kk = k * kk_ref[...]
    kk = kk / jnp.maximum(jnp.sqrt(segsum(kk * kk)), 1e-12)
    k2 = k * (1.0 + (a_lr - 1.0) * ka_ref[...])
    a_vec = -kk
    b_vec = kk * a_lr

    ri = lax.broadcasted_iota(jnp.int32, (T, T), 0)
    ci = lax.broadcasted_iota(jnp.int32, (T, T), 1)
    strict = ci < ri
    incl = ci <= ri
    eye = ci == ri
    tri_incl_bf = incl.astype(BF16)
    rc_xor = jnp.bitwise_xor(ri, ci)
    level_masks = []
    n = 1
    while n < T:
        level_masks.append(strict & (rc_xor >= n) & (rc_xor < 2 * n))
        n *= 2
    lane = lax.broadcasted_iota(jnp.int32, (T, LANES), 1)
    head0 = lane < HEAD_DIM
    pr = lax.broadcasted_iota(jnp.int32, (LANES, LANES), 0)
    pc = lax.broadcasted_iota(jnp.int32, (LANES, LANES), 1)
    same_head = (pr < HEAD_DIM) == (pc < HEAD_DIM)
    eye_p = pr == pc

    n_pairs = d_rw // LANES
    pairs = range(n_pairs)
    heads = [(p, hh) for p in pairs for hh in range(2)]
    zero = jnp.zeros((T, LANES), F32)

    cl = [_dot_exact_lhs(tri_incl_bf, lw[:, p * LANES:(p + 1) * LANES]) for p in pairs]
    rt, vb, atb, bdend_t, kdend_t, dt_row, sc = [], [], [], [], [], [], []
    for p in pairs:
        sl = slice(p * LANES, (p + 1) * LANES)
        clp = cl[p]
        cl_last = clp[T - 1:T, :]
        dinv = jnp.exp(-clp)
        dend = jnp.exp(cl_last - clp)
        at = a_vec[:, sl] * jnp.exp(clp - lw[:, sl])
        bt = b_vec[:, sl] * dinv
        kt = k2[:, sl] * dinv
        rtp = r[:, sl] * jnp.exp(clp)
        rt.append(rtp)
        vb.append(v[:, sl].astype(BF16))
        atb.append(at.astype(BF16))
        bdend_t.append((b_vec[:, sl] * dend).T.astype(BF16))
        kdend_t.append((k2[:, sl] * dend).T.astype(BF16))
        dt_row.append(jnp.exp(cl_last))
        lhs = jnp.concatenate([jnp.where(head0, at, zero), jnp.where(head0, zero, at),
                               jnp.where(head0, rtp, zero), jnp.where(head0, zero, rtp)], axis=0).astype(BF16)
        rhs = jnp.concatenate([bt, kt], axis=0).astype(BF16)
        sc.append(_dot_nt(lhs, rhs))

    aab, aak, arb, ark = {}, {}, {}, {}
    for (p, hh) in heads:
        aab[p, hh] = jnp.where(strict, sc[p][hh * T:(hh + 1) * T, 0:T], 0.0)
        aak[p, hh] = jnp.where(strict, sc[p][hh * T:(hh + 1) * T, T:2 * T], 0.0).astype(BF16)
        arb[p, hh] = jnp.where(incl, sc[p][(2 + hh) * T:(3 + hh) * T, 0:T], 0.0).astype(BF16)
        ark[p, hh] = jnp.where(incl, sc[p][(2 + hh) * T:(3 + hh) * T, T:2 * T], 0.0).astype(BF16)

    tinv = {hd: jnp.where(eye, 1.0, jnp.where(level_masks[0], aab[hd], 0.0)).astype(BF16) for hd in heads}
    for lm in level_masks[1:]:
        half = {hd: _dot(tinv[hd], jnp.where(lm, aab[hd], 0.0).astype(BF16)).astype(BF16) for hd in heads}
        tinv = {hd: (tinv[hd].astype(F32) + _dot(half[hd], tinv[hd])).astype(BF16) for hd in heads}

    aakv = {hd: _dot(aak[hd], vb[hd[0]]).astype(BF16) for hd in heads}
    ap_h = {hd: _dot(tinv[hd], atb[hd[0]]) for hd in heads}
    w2_h = {hd: _dot(tinv[hd], aakv[hd]) for hd in heads}
    apn = [jnp.where(head0, ap_h[p, 0], ap_h[p, 1]).astype(BF16) for p in pairs]
    w2 = [jnp.where(head0, w2_h[p, 0], w2_h[p, 1]).astype(BF16) for p in pairs]
    qp_h = {hd: _dot(arb[hd], apn[hd[0]]) for hd in heads}
    yl_h = {hd: _dot(arb[hd], w2[hd[0]]) + _dot(ark[hd], vb[hd[0]]) for hd in heads}

    ys = []
    for p in pairs:
        qp = rt[p] + jnp.where(head0, qp_h[p, 0], qp_h[p, 1])
        yl = jnp.where(head0, yl_h[p, 0], yl_h[p, 1])
        g_mat = jnp.where(same_head, _dot(bdend_t[p], apn[p]), 0.0) + jnp.where(eye_p, dt_row[p], 0.0)
        c_mat = jnp.where(same_head, _dot(bdend_t[p], w2[p]) + _dot(kdend_t[p], vb[p]), 0.0)
        m0 = m_ref[p].astype(BF16)
        ys.append(_dot(qp.astype(BF16), m0) + yl)
        m_ref[p] = _dot(g_mat.astype(BF16), m0) + c_mat

    y = jnp.concatenate(ys, axis=1)
    inv_n = 1.0 / HEAD_DIM
    mean = segsum(y) * inv_n
    yc = y - mean
    var = segsum(yc * yc) * inv_n
    yn = yc * lax.rsqrt(var + GN_EPS) * lng_ref[...] + lnb_ref[...]
    bonus = segsum(r * k2 * rk_ref[...]) * v
    o_ref[...] = ((yn + bonus) * g).astype(BF16)


def _wkv(feat_rw, mu, w0, wlw, a0, wla, wlg, k_k, k_a, r_k, lnx_g, lnx_b, bd, d_rw):
    B, Lp, c_rw = feat_rw.shape
    T = WKV_CHUNK
    vec = _const_spec((1, d_rw))
    return pl.pallas_call(
        functools.partial(_wkv_kernel, d_rw=d_rw),
        grid=(B, Lp // T),
        in_specs=[
            pl.BlockSpec((None, T, c_rw), lambda b, c: (b, c, 0)),
            _const_spec((1, c_rw)),
            vec, _const_spec((LANES, d_rw)), vec, _const_spec((LANES, d_rw)), _const_spec((LORA_G, d_rw)),
            vec, vec, vec, vec, vec,
            _const_spec((d_rw, d_rw)),
        ],
        out_specs=pl.BlockSpec((None, T, d_rw), lambda b, c: (b, c, 0)),
        out_shape=jax.ShapeDtypeStruct((B, Lp, d_rw), BF16),
        scratch_shapes=[pltpu.VMEM((d_rw // LANES, LANES, LANES), F32),
                        pltpu.VMEM((1, c_rw), F32)],
        compiler_params=_params(("parallel", "arbitrary")),
        name="wkv",
    )(feat_rw, mu, w0, wlw, a0, wla, wlg, k_k, k_a, r_k, lnx_g, lnx_b, bd)


def _fxprep_kernel(fx_ref, qg_ref, kg_ref, bf_ref, sq_ref, sk_ref, cq_ref, ck_ref, cv_ref,
                   q_ref, kt_ref, v_ref, f2_ref, carry_ref, *, n_heads):
    i = pl.program_id(1)

    @pl.when(i == 0)
    def _():
        carry_ref[...] = jnp.zeros_like(carry_ref)

    hw = n_heads * LANES
    x = fx_ref[...]
    tm = x.shape[0]
    fl = x[:, 3 * hw:3 * hw + LANES]
    z = fl + bf_ref[...]
    log_f = jnp.minimum(z, 0.0) - jnp.log(1.0 + jnp.exp(-jnp.abs(z)))
    lane = lax.broadcasted_iota(jnp.int32, (tm, LANES), 1)
    log_f = jnp.where(lane < n_heads, log_f, 0.0)
    ri = lax.broadcasted_iota(jnp.int32, (tm, tm), 0)
    ci = lax.broadcasted_iota(jnp.int32, (tm, tm), 1)
    tri = (ci <= ri).astype(BF16)
    fcum = carry_ref[...] + _dot_exact_lhs(tri, log_f)
    carry_ref[...] = fcum[tm - 1:tm, :]
    f2 = fcum * LOG2E
    f2_ref[...] = f2
    hi, mid, lo = _split3(f2)
    parts = jnp.concatenate([hi, mid, lo], axis=1)
    extra_q = _dot(parts, sq_ref[...]) + cq_ref[...]
    extra_k = _dot(parts, sk_ref[...]) + ck_ref[...]

    inv_n = 1.0 / HEAD_DIM
    for h in range(n_heads):
        sl = slice(h * LANES, (h + 1) * LANES)
        qh = x[:, sl]
        kh = x[:, hw + h * LANES: hw + (h + 1) * LANES]
        qn = qh * lax.rsqrt(jnp.sum(qh * qh, axis=-1, keepdims=True) * inv_n + NORM_EPS) * qg_ref[:, sl]
        kn = kh * lax.rsqrt(jnp.sum(kh * kh, axis=-1, keepdims=True) * inv_n + NORM_EPS) * kg_ref[:, sl]
        q_ref[:, sl] = (qn + extra_q[:, sl]).astype(BF16)
        kt_ref[sl, :] = (kn + extra_k[:, sl]).T.astype(BF16)
    v_ref[...] = (x[:, 2 * hw:3 * hw] + cv_ref[...]).astype(BF16)


def _fxprep(feat_fx, qg, kg, bf, sq, sk, cq, ck, cv, n_heads, tm):
    B, Lp, n_fx = feat_fx.shape
    hw = n_heads * LANES
    return pl.pallas_call(
        functools.partial(_fxprep_kernel, n_heads=n_heads),
        grid=(B, Lp // tm),
        in_specs=[
            pl.BlockSpec((None, tm, n_fx), lambda b, i: (b, i, 0)),
            _const_spec((1, hw)), _const_spec((1, hw)), _const_spec((1, LANES)),
            _const_spec((3 * LANES, hw)), _const_spec((3 * LANES, hw)),
            _const_spec((1, hw)), _const_spec((1, hw)), _const_spec((1, hw)),
        ],
        out_specs=[
            pl.BlockSpec((None, tm, hw), lambda b, i: (b, i, 0)),
            pl.BlockSpec((None, hw, tm), lambda b, i: (b, 0, i)),
            pl.BlockSpec((None, tm, hw), lambda b, i: (b, i, 0)),
            pl.BlockSpec((None, tm, LANES), lambda b, i: (b, i, 0)),
        ],
        out_shape=[
            jax.ShapeDtypeStruct((B, Lp, hw), BF16),
            jax.ShapeDtypeStruct((B, hw, Lp), BF16),
            jax.ShapeDtypeStruct((B, Lp, hw), BF16),
            jax.ShapeDtypeStruct((B, Lp, LANES), F32),
        ],
        scratch_shapes=[pltpu.VMEM((1, LANES), F32)],
        compiler_params=_params(("parallel", "arbitrary")),
        name="fxprep",
    )(feat_fx, qg, kg, bf, sq, sk, cq, ck, cv)


def _attn_kernel(jlo_ref, q_ref, kt_ref, v_ref, o_ref, *, blk):
    b, pr, i = pl.program_id(0), pl.program_id(1), pl.program_id(2)
    lo = jlo_ref[(b * pl.num_programs(1) + pr) * pl.num_programs(2) + i]
    ri = lax.broadcasted_iota(jnp.int32, (blk, blk), 0)
    ci = lax.broadcasted_iota(jnp.int32, (blk, blk), 1)
    causal = ci <= ri
    lane = lax.broadcasted_iota(jnp.int32, (blk, LANES), 1)
    hs = [slice(hh * LANES, (hh + 1) * LANES) for hh in range(2)]
    q = [q_ref[:, sl] for sl in hs]

    def scores(j):
        off = pl.multiple_of(j * blk, blk)
        return tuple(_dot(q[hh], kt_ref[hs[hh], pl.ds(off, blk)]) for hh in range(2))

    def body(n, carry):
        s_cur, state = carry
        j = i - n
        s_next = scores(jnp.maximum(j - 1, 0))
        off = pl.multiple_of(j * blk, blk)
        keep = causal | (n > 0)
        probs, stats = [], []
        for hh in range(2):
            m, _ = state[hh]
            s = jnp.where(keep, s_cur[hh], NEG_BIG)
            m_new = jnp.maximum(m, jnp.max(s, axis=-1, keepdims=True))
            probs.append(jnp.exp2(s - m_new).astype(BF16))
            stats.append((m_new, jnp.exp2(m - m_new)))
        new_state = []
        for hh in range(2):
            m_new, alpha = stats[hh]
            acc = alpha * state[hh][1] + _dot(probs[hh], v_ref[pl.ds(off, blk), hs[hh]])
            new_state.append((m_new, acc))
        return s_next, tuple(new_state)

    init = (jnp.full((blk, 1), NEG_BIG, F32), jnp.zeros((blk, LANES), F32))
    _, state = lax.fori_loop(0, i - lo + 1, body, (scores(i), (init, init)))
    outs = []
    for hh in range(2):
        acc = state[hh][1]
        den_lane = HEAD_DIM if hh == 0 else 0
        outs.append(acc / acc[:, den_lane:den_lane + 1])
    o_ref[...] = jnp.where(lane < HEAD_DIM, outs[0], outs[1]).astype(BF16)


def _attn(jlo, qp, ktp, vp, blk):
    B, Lp, hw = qp.shape
    n_pairs = hw // (2 * LANES)
    grid_spec = pltpu.PrefetchScalarGridSpec(
        num_scalar_prefetch=1,
        grid=(B, n_pairs, Lp // blk),
        in_specs=[
            pl.BlockSpec((None, blk, 2 * LANES), lambda b, p, i, jlo_ref: (b, i, p)),
            pl.BlockSpec((None, 2 * LANES, Lp), lambda b, p, i, jlo_ref: (b, p, 0)),
            pl.BlockSpec((None, Lp, 2 * LANES), lambda b, p, i, jlo_ref: (b, 0, p)),
        ],
        out_specs=pl.BlockSpec((None, blk, LANES), lambda b, p, i, jlo_ref: (b, i, p)),
    )
    return pl.pallas_call(
        functools.partial(_attn_kernel, blk=blk),
        grid_spec=grid_spec,
        out_shape=jax.ShapeDtypeStruct((B, Lp, n_pairs * LANES), BF16),
        compiler_params=_params(("parallel", "parallel", "arbitrary")),
        name="attn",
    )(jlo, qp, ktp, vp)


def _first_live_block(f2, cmax, blk, n_heads):
    B, Lp, _ = f2.shape
    nq = Lp // blk
    f_first = f2[:, 0::blk, :n_heads]
    f_last = f2[:, blk - 1::blk, :n_heads]
    gap = f_first[:, :, None, :] - f_last[:, None, :, :]
    jj = jnp.arange(nq)
    live = (gap + 2.0 * cmax >= -SKIP_LOG2) | (jj[None, None, :, None] >= jj[None, :, None, None])
    jlo = jnp.argmax(live, axis=2).astype(jnp.int32)
    jlo = jnp.min(jlo.reshape(B, nq, n_heads // 2, 2), axis=3)
    return jnp.transpose(jlo, (0, 2, 1)).reshape(-1)


def _outmlp_kernel(h_ref, yrw_ref, yfx_ref, gate_ref, wrw_ref, wfx_ref, wo_ref, gm_ref, wup_ref,
                   wdn_ref, o_ref, *, ff_chunk):
    d = h_ref.shape[1]
    gates = gate_ref[...].astype(F32)
    merged = gates[:, :d] * _dot(yrw_ref[...], wrw_ref[...]) + gates[:, d:] * _dot(yfx_ref[...], wfx_ref[...])
    h1 = h_ref[...] + _dot(merged.astype(BF16), wo_ref[...])
    ms = jnp.mean(h1 * h1, axis=-1, keepdims=True)
    z = (h1 * lax.rsqrt(ms + NORM_EPS) * gm_ref[...]).astype(BF16)
    acc = h1
    for c0 in range(0, wup_ref.shape[1], ff_chunk):
        u = jnp.maximum(_dot(z, wup_ref[:, c0:c0 + ff_chunk]), 0.0)
        acc = acc + _dot((u * u).astype(BF16), wdn_ref[c0:c0 + ff_chunk, :])
    o_ref[...] = acc


def _outmlp(h2, yrw, yfx, gates, wrw, wfx, wo, gm, wup, wdn, tm):
    rows, d = h2.shape
    d_rw, d_fx, d_ff = wrw.shape[0], wfx.shape[0], wup.shape[1]
    return pl.pallas_call(
        functools.partial(_outmlp_kernel, ff_chunk=min(d_ff, 1024)),
        grid=(rows // tm,),
        in_specs=[
            pl.BlockSpec((tm, d), lambda i: (i, 0)),
            pl.BlockSpec((tm, d_rw), lambda i: (i, 0)),
            pl.BlockSpec((tm, d_fx), lambda i: (i, 0)),
            pl.BlockSpec((tm, 2 * d), lambda i: (i, 0)),
            _const_spec((d_rw, d)), _const_spec((d_fx, d)), _const_spec((d, d)),
            _const_spec((1, d)), _const_spec((d, d_ff)), _const_spec((d_ff, d)),
        ],
        out_specs=pl.BlockSpec((tm, d), lambda i: (i, 0)),
        out_shape=jax.ShapeDtypeStruct((rows, d), F32),
        compiler_params=_params(("parallel",)),
        name="outmlp",
    )(h2, yrw, yfx, gates, wrw, wfx, wo, gm, wup, wdn)


def _largest_divisor(n, candidates):
    for c in candidates:
        if n % c == 0:
            return c
    raise ValueError(f"no tile in {candidates} divides {n}")


def _head_pad_cols(w, n_heads, upper_for_odd):
    kdim = w.shape[0]
    w3 = w.reshape(kdim, n_heads, HEAD_DIM)
    z = jnp.zeros_like(w3)
    lower = jnp.concatenate([w3, z], axis=2)
    if not upper_for_odd:
        return lower.reshape(kdim, n_heads * LANES)
    upper = jnp.concatenate([z, w3], axis=2)
    odd = (jnp.arange(n_heads) % 2 == 1)[None, :, None]
    return jnp.where(odd, upper, lower).reshape(kdim, n_heads * LANES)


def _attn_constants(n_heads):
    hw = n_heads * LANES
    sq = jnp.zeros((3 * LANES, hw), F32)
    sk = jnp.zeros((3 * LANES, hw), F32)
    cq = jnp.zeros((1, hw), F32)
    ck = jnp.zeros((1, hw), F32)
    cv = jnp.zeros((1, hw), F32)
    for h in range(n_heads):
        base = h * LANES + HEAD_DIM
        for part in range(3):
            sq = sq.at[part * LANES + h, base + part].set(1.0)
            sk = sk.at[part * LANES + h, base + 3 + part].set(-1.0)
            cq = cq.at[0, base + 3 + part].set(1.0)
            ck = ck.at[0, base + part].set(1.0)
        cv = cv.at[0, h * LANES + (HEAD_DIM if h % 2 == 0 else 0)].set(1.0)
    return sq.astype(BF16), sk.astype(BF16), cq, ck, cv


def kernel(x, meta, norm_mix, w_in, b_gate, b_f, tm_mu, w0, w_lora_up, a0, a_lora_up, g_lora_up, k_k, k_a,
           r_k, lnx_g, lnx_b, q_gain, k_gain, w_out_rw, w_out_fx, w_o, norm_mlp, w_up, w_down):
    B, S, D = x.shape
    depth = w_in.shape[0]
    d_rw = w0.shape[1]
    d_fx = w_out_fx.shape[1]
    h_fx = b_f.shape[1]
    c_rw = 3 * d_rw + LORA_W + LORA_A + LORA_G
    c_fx = 3 * d_fx + h_fx
    assert d_rw % LANES == 0 and h_fx % 2 == 0 and d_fx == h_fx * HEAD_DIM and h_fx <= LANES
    assert LORA_W + LORA_A == LANES and LORA_G == LANES and HEAD_DIM * 2 == LANES

    L = N_META + S
    Lp = -(-L // SEQ_ALIGN) * SEQ_ALIGN
    rows = B * Lp
    tm_row = _largest_divisor(rows, (256, 128))
    tm_prep = _largest_divisor(Lp, (256, 128))
    blk = SEQ_ALIGN
    hw = h_fx * LANES

    meta_b = jnp.broadcast_to(meta.astype(x.dtype)[None], (B, N_META, D))
    h = jnp.concatenate([meta_b, x, jnp.zeros((B, Lp - L, D), x.dtype)], axis=1).reshape(rows, D)

    sq, sk, cq, ck, cv = _attn_constants(h_fx)
    seg = jnp.arange(d_rw) // HEAD_DIM
    bd = (seg[:, None] == seg[None, :]).astype(BF16)
    zeros_lora = jnp.zeros((LORA_W, d_rw), F32)
    scale = HEAD_DIM ** -0.5
    row = lambda t: t.reshape(1, -1)

    for l in range(depth):
        wl = w_in[l]
        w_rw = wl[:, :c_rw].astype(BF16)
        fx0 = c_rw
        w_fx = jnp.concatenate([
            _head_pad_cols(wl[:, fx0:fx0 + d_fx], h_fx, False),
            _head_pad_cols(wl[:, fx0 + d_fx:fx0 + 2 * d_fx], h_fx, False),
            _head_pad_cols(wl[:, fx0 + 2 * d_fx:fx0 + 3 * d_fx], h_fx, True),
            jnp.pad(wl[:, fx0 + 3 * d_fx:fx0 + c_fx], ((0, 0), (0, LANES - h_fx))),
        ], axis=1).astype(BF16)
        w_g = wl[:, c_rw + c_fx:].astype(BF16)

        feat_rw, feat_fx, gates = _inproj(h, row(norm_mix[l]), w_rw, w_fx, w_g, row(b_gate[l]), tm_row)

        y_rw = _wkv(
            feat_rw.reshape(B, Lp, c_rw), row(tm_mu[l]), row(w0[l]),
            jnp.concatenate([w_lora_up[l], zeros_lora], axis=0).astype(BF16), row(a0[l]),
            jnp.concatenate([zeros_lora, a_lora_up[l]], axis=0).astype(BF16), g_lora_up[l].astype(BF16),
            row(k_k[l]), row(k_a[l]), row(r_k[l]), row(lnx_g[l]), row(lnx_b[l]), bd, d_rw)

        qg = _head_pad_cols(jnp.tile(q_gain[l] * (scale * LOG2E), h_fx)[None, :], h_fx, False)
        kg = _head_pad_cols(jnp.tile(k_gain[l], h_fx)[None, :], h_fx, False)
        bf = jnp.pad(b_f[l], (0, LANES - h_fx))[None, :]
        qp, ktp, vp, f2 = _fxprep(feat_fx.reshape(B, Lp, -1), qg, kg, bf, sq, sk, cq, ck, cv, h_fx, tm_prep)
        cmax = 1.02 * HEAD_DIM * jnp.max(jnp.abs(qg)) * jnp.max(jnp.abs(kg))
        y_fx = _attn(_first_live_block(f2, cmax, blk, h_fx), qp, ktp, vp, blk)

        h = _outmlp(h, y_rw.reshape(rows, d_rw), y_fx.reshape(rows, d_fx), gates,
                    w_out_rw[l].astype(BF16), w_out_fx[l].astype(BF16), w_o[l].astype(BF16),
                    row(norm_mlp[l]), w_up[l].astype(BF16), w_down[l].astype(BF16), tm_row)

    return h.reshape(B, Lp, D)[:, N_META:N_META + S]
```

```python
import functools

import jax
import jax.numpy as jnp
from jax import lax
from jax.experimental import pallas as pl
from jax.experimental.pallas import tpu as pltpu

F32 = jnp.float32
BF16 = jnp.bfloat16

N_META = 16
HEAD_DIM = 64
LORA_W = 64
LORA_A = 64
LORA_G = 128
NORM_EPS = 1e-6
GN_EPS = 64e-5
DECAY_OFFSET = 0.5
LOG2E = 1.4426950408889634

LANES = 128
SEQ_ALIGN = 256
SKIP_LOG2 = 160.0
WKV_CHUNK = 128
ATTN_GROUP = 4
FIXED_SHIFT_MAX_LOG2 = 50.0
VMEM_LIMIT_BYTES = 56 * 1024 * 1024
NEG_BIG = -1e30


def _dot(a, b):
    return jnp.dot(a, b, preferred_element_type=F32)


def _dot_nt(a, b):
    return lax.dot_general(a, b, (((1,), (1,)), ((), ())), preferred_element_type=F32)


def _split2(x):
    hi = x.astype(BF16)
    lo = (x - hi.astype(F32)).astype(BF16)
    return hi, lo


def _split3(x):
    hi = x.astype(BF16)
    r1 = x - hi.astype(F32)
    mid = r1.astype(BF16)
    lo = (r1 - mid.astype(F32)).astype(BF16)
    return hi, mid, lo


def _dot_exact_lhs(m_bf16, x):
    hi, mid, lo = _split3(x)
    return _dot(m_bf16, hi) + _dot(m_bf16, mid) + _dot(m_bf16, lo)


def _sigmoid(x):
    return 1.0 / (1.0 + jnp.exp(-x))


def _softplus(x):
    return jnp.maximum(x, 0.0) + jnp.log(1.0 + jnp.exp(-jnp.abs(x)))


def _const_spec(shape):
    nd = len(shape)
    return pl.BlockSpec(shape, lambda *_: (0,) * nd)


def _params(sem):
    return pltpu.CompilerParams(dimension_semantics=sem, vmem_limit_bytes=VMEM_LIMIT_BYTES)


def _rmsnorm_bf16(x, g):
    ms = jnp.mean(x * x, axis=-1, keepdims=True)
    return (x * lax.rsqrt(ms + NORM_EPS) * g).astype(BF16)


def _inproj_kernel(h_ref, g_ref, wrw_ref, wg_ref, bg_ref, rw_ref, gate_ref):
    u = _rmsnorm_bf16(h_ref[...], g_ref[...])
    rw_ref[...] = _dot(u, wrw_ref[...])
    gate_ref[...] = _sigmoid(_dot(u, wg_ref[...]) + bg_ref[...]).astype(BF16)


def _inproj(h2, g, wrw, wg, bg, tm):
    rows, d = h2.shape
    n_rw, n_g = wrw.shape[1], wg.shape[1]
    return pl.pallas_call(
        _inproj_kernel,
        grid=(rows // tm,),
        in_specs=[
            pl.BlockSpec((tm, d), lambda i: (i, 0)),
            _const_spec((1, d)),
            _const_spec((d, n_rw)),
            _const_spec((d, n_g)),
            _const_spec((1, n_g)),
        ],
        out_specs=[
            pl.BlockSpec((tm, n_rw), lambda i: (i, 0)),
            pl.BlockSpec((tm, n_g), lambda i: (i, 0)),
        ],
        out_shape=[
            jax.ShapeDtypeStruct((rows, n_rw), F32),
            jax.ShapeDtypeStruct((rows, n_g), BF16),
        ],
        compiler_params=_params(("parallel",)),
        name="inproj",
    )(h2, g, wrw, wg, bg)


def _wkv_kernel(x_ref, mu_ref, w0_ref, wlw_ref, a0_ref, wla_ref, wlg_ref, kk_ref, ka_ref, rk_ref,
                lng_ref, lnb_ref, bd_ref, o_ref, m_ref, prev_ref, *, d_rw):
    T = WKV_CHUNK
    c = pl.program_id(1)

    @pl.when(c == 0)
    def _():
        m_ref[...] = jnp.zeros_like(m_ref)
        prev_ref[...] = jnp.zeros_like(prev_ref)

    x = x_ref[...]
    row = lax.broadcasted_iota(jnp.int32, x.shape, 0)
    shifted = jnp.where(row == 0, prev_ref[...], pltpu.roll(x, 1, 0))
    prev_ref[...] = x[T - 1:T, :]
    f = x + mu_ref[...] * (shifted - x)

    r = f[:, 0:d_rw]
    k = f[:, d_rw:2 * d_rw]
    v = f[:, 2 * d_rw:3 * d_rw]
    wa_d = f[:, 3 * d_rw:3 * d_rw + LORA_W + LORA_A]
    g_d = f[:, 3 * d_rw + LORA_W + LORA_A:]

    bd = bd_ref[...]
    n_pairs = d_rw // LANES

    def segsum(t):
        stacked = jnp.concatenate([t[:, p * LANES:(p + 1) * LANES] for p in range(n_pairs)], axis=0)
        hi, lo = _split2(stacked)
        out = _dot(hi, bd) + _dot(lo, bd)
        return jnp.concatenate([out[p * T:(p + 1) * T] for p in range(n_pairs)], axis=1)

    w_log = -_softplus(-(w0_ref[...] + _dot(jnp.tanh(wa_d).astype(BF16), wlw_ref[...]))) - DECAY_OFFSET
    lw = -jnp.exp(w_log)
    a_lr = _sigmoid(a0_ref[...] + _dot(wa_d.astype(BF16), wla_ref[...]))
    g = _dot(_sigmoid(g_d).astype(BF16), wlg_ref[...])
    kk = k * kk_ref[...]
    kk = kk / jnp.maximum(jnp.sqrt(segsum(kk * kk)), 1e-12)
    k2 = k * (1.0 + (a_lr - 1.0) * ka_ref[...])
    a_vec = -kk
    b_vec = kk * a_lr

    ri = lax.broadcasted_iota(jnp.int32, (T, T), 0)
    ci = lax.broadcasted_iota(jnp.int32, (T, T), 1)
    strict = ci < ri
    incl = ci <= ri
    eye = ci == ri
    tri_incl_bf = incl.astype(BF16)
    rc_xor = jnp.bitwise_xor(ri, ci)
    level_masks = []
    n = 1
    while n < T:
        level_masks.append(strict & (rc_xor >= n) & (rc_xor < 2 * n))
        n *= 2
    lane = lax.broadcasted_iota(jnp.int32, (T, LANES), 1)
    head0 = lane < HEAD_DIM
    pr = lax.broadcasted_iota(jnp.int32, (LANES, LANES), 0)
    pc = lax.broadcasted_iota(jnp.int32, (LANES, LANES), 1)
    same_head = (pr < HEAD_DIM) == (pc < HEAD_DIM)
    eye_p = pr == pc

    pairs = range(n_pairs)
    heads = [(p, hh) for p in pairs for hh in range(2)]
    zero = jnp.zeros((T, LANES), F32)

    cl = [_dot_exact_lhs(tri_incl_bf, lw[:, p * LANES:(p + 1) * LANES]) for p in pairs]
    rt, vb, atb, bdend_t, kdend_t, dt_row, sc = [], [], [], [], [], [], []
    for p in pairs:
        sl = slice(p * LANES, (p + 1) * LANES)
        clp = cl[p]
        cl_last = clp[T - 1:T, :]
        dinv = jnp.exp(-clp)
        dend = jnp.exp(cl_last - clp)
        at = a_vec[:, sl] * jnp.exp(clp - lw[:, sl])
        bt = b_vec[:, sl] * dinv
        kt = k2[:, sl] * dinv
        rtp = r[:, sl] * jnp.exp(clp)
        rt.append(rtp)
        vb.append(v[:, sl].astype(BF16))
        atb.append(at.astype(BF16))
        bdend_t.append((b_vec[:, sl] * dend).T.astype(BF16))
        kdend_t.append((k2[:, sl] * dend).T.astype(BF16))
        dt_row.append(jnp.exp(cl_last))
        lhs = jnp.concatenate([jnp.where(head0, at, zero), jnp.where(head0, zero, at),
                               jnp.where(head0, rtp, zero), jnp.where(head0, zero, rtp)], axis=0).astype(BF16)
        rhs = jnp.concatenate([bt, kt], axis=0).astype(BF16)
        sc.append(_dot_nt(lhs, rhs))

    aab, aak, arb, ark = {}, {}, {}, {}
    for (p, hh) in heads:
        aab[p, hh] = jnp.where(strict, sc[p][hh * T:(hh + 1) * T, 0:T], 0.0)
        aak[p, hh] = jnp.where(strict, sc[p][hh * T:(hh + 1) * T, T:2 * T], 0.0).astype(BF16)
        arb[p, hh] = jnp.where(incl, sc[p][(2 + hh) * T:(3 + hh) * T, 0:T], 0.0).astype(BF16)
        ark[p, hh] = jnp.where(incl, sc[p][(2 + hh) * T:(3 + hh) * T, T:2 * T], 0.0).astype(BF16)

    tinv = {hd: jnp.where(eye, 1.0, jnp.where(level_masks[0], aab[hd], 0.0)).astype(BF16) for hd in heads}
    for lm in level_masks[1:]:
        half = {hd: _dot(tinv[hd], jnp.where(lm, aab[hd], 0.0).astype(BF16)).astype(BF16) for hd in heads}
        tinv = {hd: (tinv[hd].astype(F32) + _dot(half[hd], tinv[hd])).astype(BF16) for hd in heads}

    aakv = {hd: _dot(aak[hd], vb[hd[0]]).astype(BF16) for hd in heads}
    ap_h = {hd: _dot(tinv[hd], atb[hd[0]]) for hd in heads}
    w2_h = {hd: _dot(tinv[hd], aakv[hd]) for hd in heads}
    apn = [jnp.where(head0, ap_h[p, 0], ap_h[p, 1]).astype(BF16) for p in pairs]
    w2 = [jnp.where(head0, w2_h[p, 0], w2_h[p, 1]).astype(BF16) for p in pairs]
    qp_h = {hd: _dot(arb[hd], apn[hd[0]]) for hd in heads}
    yl_h = {hd: _dot(arb[hd], w2[hd[0]]) + _dot(ark[hd], vb[hd[0]]) for hd in heads}

    ys = []
    for p in pairs:
        qp = rt[p] + jnp.where(head0, qp_h[p, 0], qp_h[p, 1])
        yl = jnp.where(head0, yl_h[p, 0], yl_h[p, 1])
        g_mat = jnp.where(same_head, _dot(bdend_t[p], apn[p]), 0.0) + jnp.where(eye_p, dt_row[p], 0.0)
        c_mat = jnp.where(same_head, _dot(bdend_t[p], w2[p]) + _dot(kdend_t[p], vb[p]), 0.0)
        m0 = m_ref[p].astype(BF16)
        ys.append(_dot(qp.astype(BF16), m0) + yl)
        m_ref[p] = _dot(g_mat.astype(BF16), m0) + c_mat

    y = jnp.concatenate(ys, axis=1)
    inv_n = 1.0 / HEAD_DIM
    mean = segsum(y) * inv_n
    yc = y - mean
    var = segsum(yc * yc) * inv_n
    yn = yc * lax.rsqrt(var + GN_EPS) * lng_ref[...] + lnb_ref[...]
    bonus = segsum(r * k2 * rk_ref[...]) * v
    o_ref[...] = ((yn + bonus) * g).astype(BF16)


def _wkv(feat_rw, mu, w0, wlw, a0, wla, wlg, k_k, k_a, r_k, lnx_g, lnx_b, bd, d_rw):
    B, Lp, c_rw = feat_rw.shape
    T = WKV_CHUNK
    vec = _const_spec((1, d_rw))
    return pl.pallas_call(
        functools.partial(_wkv_kernel, d_rw=d_rw),
        grid=(B, Lp // T),
        in_specs=[
            pl.BlockSpec((None, T, c_rw), lambda b, c: (b, c, 0)),
            _const_spec((1, c_rw)),
            vec, _const_spec((LANES, d_rw)), vec, _const_spec((LANES, d_rw)), _const_spec((LORA_G, d_rw)),
            vec, vec, vec, vec, vec,
            _const_spec((LANES, LANES)),
        ],
        out_specs=pl.BlockSpec((None, T, d_rw), lambda b, c: (b, c, 0)),
        out_shape=jax.ShapeDtypeStruct((B, Lp, d_rw), BF16),
        scratch_shapes=[pltpu.VMEM((d_rw // LANES, LANES, LANES), F32),
                        pltpu.VMEM((1, c_rw), F32)],
        compiler_params=_params(("parallel", "arbitrary")),
        name="wkv",
    )(feat_rw, mu, w0, wlw, a0, wla, wlg, k_k, k_a, r_k, lnx_g, lnx_b, bd)


def _fxprep_kernel(h_ref, g_ref, wfx_ref, qg_ref, kg_ref, bf_ref, sq_ref, sk_ref, cq_ref, ck_ref, cv_ref,
                   q_ref, kt_ref, v_ref, f2_ref, carry_ref, *, n_heads):
    i = pl.program_id(1)

    @pl.when(i == 0)
    def _():
        carry_ref[...] = jnp.zeros_like(carry_ref)

    hw = n_heads * LANES
    x = _dot(_rmsnorm_bf16(h_ref[...], g_ref[...]), wfx_ref[...])
    tm = x.shape[0]
    fl = x[:, 3 * hw:3 * hw + LANES]
    z = fl + bf_ref[...]
    log_f = jnp.minimum(z, 0.0) - jnp.log(1.0 + jnp.exp(-jnp.abs(z)))
    lane = lax.broadcasted_iota(jnp.int32, (tm, LANES), 1)
    log_f = jnp.where(lane < n_heads, log_f, 0.0)
    ri = lax.broadcasted_iota(jnp.int32, (tm, tm), 0)
    ci = lax.broadcasted_iota(jnp.int32, (tm, tm), 1)
    tri = (ci <= ri).astype(BF16)
    fcum = carry_ref[...] + _dot_exact_lhs(tri, log_f)
    carry_ref[...] = fcum[tm - 1:tm, :]
    f2 = fcum * LOG2E
    f2_ref[...] = f2
    hi, mid, lo = _split3(f2)
    parts = jnp.concatenate([hi, mid, lo], axis=1)
    extra_q = _dot(parts, sq_ref[...]) + cq_ref[...]
    extra_k = _dot(parts, sk_ref[...]) + ck_ref[...]

    inv_n = 1.0 / HEAD_DIM
    for h in range(n_heads):
        sl = slice(h * LANES, (h + 1) * LANES)
        qh = x[:, sl]
        kh = x[:, hw + h * LANES: hw + (h + 1) * LANES]
        qn = qh * lax.rsqrt(jnp.sum(qh * qh, axis=-1, keepdims=True) * inv_n + NORM_EPS) * qg_ref[:, sl]
        kn = kh * lax.rsqrt(jnp.sum(kh * kh, axis=-1, keepdims=True) * inv_n + NORM_EPS) * kg_ref[:, sl]
        q_ref[:, sl] = (qn + extra_q[:, sl]).astype(BF16)
        kt_ref[sl, :] = (kn + extra_k[:, sl]).T.astype(BF16)
    v_ref[...] = (x[:, 2 * hw:3 * hw] + cv_ref[...]).astype(BF16)


def _fxprep(h3, g, wfx, qg, kg, bf, sq, sk, cq, ck, cv, n_heads, tm):
    B, Lp, d = h3.shape
    n_fx = wfx.shape[1]
    hw = n_heads * LANES
    return pl.pallas_call(
        functools.partial(_fxprep_kernel, n_heads=n_heads),
        grid=(B, Lp // tm),
        in_specs=[
            pl.BlockSpec((None, tm, d), lambda b, i: (b, i, 0)),
            _const_spec((1, d)), _const_spec((d, n_fx)),
            _const_spec((1, hw)), _const_spec((1, hw)), _const_spec((1, LANES)),
            _const_spec((3 * LANES, hw)), _const_spec((3 * LANES, hw)),
            _const_spec((1, hw)), _const_spec((1, hw)), _const_spec((1, hw)),
        ],
        out_specs=[
            pl.BlockSpec((None, tm, hw), lambda b, i: (b, i, 0)),
            pl.BlockSpec((None, hw, tm), lambda b, i: (b, 0, i)),
            pl.BlockSpec((None, tm, hw), lambda b, i: (b, i, 0)),
            pl.BlockSpec((None, tm, LANES), lambda b, i: (b, i, 0)),
        ],
        out_shape=[
            jax.ShapeDtypeStruct((B, Lp, hw), BF16),
            jax.ShapeDtypeStruct((B, hw, Lp), BF16),
            jax.ShapeDtypeStruct((B, Lp, hw), BF16),
            jax.ShapeDtypeStruct((B, Lp, LANES), F32),
        ],
        scratch_shapes=[pltpu.VMEM((1, LANES), F32)],
        compiler_params=_params(("parallel", "arbitrary")),
        name="fxprep",
    )(h3, g, wfx, qg, kg, bf, sq, sk, cq, ck, cv)


def _attn_kernel(jlo_ref, q_ref, kt_ref, v_ref, o_ref, acc_scr, m_scr, *, blk, nh, online):
    b, g, i = pl.program_id(0), pl.program_id(1), pl.program_id(2)
    lo = jlo_ref[(b * pl.num_programs(1) + g) * pl.num_programs(2) + i]
    ri = lax.broadcasted_iota(jnp.int32, (blk, blk), 0)
    ci = lax.broadcasted_iota(jnp.int32, (blk, blk), 1)
    causal = ci <= ri
    lane = lax.broadcasted_iota(jnp.int32, (blk, LANES), 1)
    hs = [slice(h * LANES, (h + 1) * LANES) for h in range(nh)]

    def block(j, diagonal):
        off = pl.multiple_of(j * blk, blk)
        s = [_dot(q_ref[:, hs[h]], kt_ref[hs[h], pl.ds(off, blk)]) for h in range(nh)]
        probs, alphas = [], []
        for h in range(nh):
            sm = jnp.where(causal, s[h], NEG_BIG) if diagonal else s[h]
            if online:
                row_max = jnp.max(sm, axis=-1, keepdims=True)
                if diagonal:
                    m_new = row_max
                else:
                    m_old = m_scr[h]
                    m_new = jnp.maximum(m_old, row_max)
                    alphas.append(jnp.exp2(m_old - m_new))
                m_scr[h] = m_new
                sm = sm - m_new
            probs.append(jnp.exp2(sm).astype(BF16))
        for h in range(nh):
            pv = _dot(probs[h], v_ref[pl.ds(off, blk), hs[h]])
            if diagonal:
                acc_scr[h] = pv
            elif online:
                acc_scr[h] = alphas[h] * acc_scr[h] + pv
            else:
                acc_scr[h] += pv

    block(i, True)

    def body(n, carry):
        block(i - 1 - n, False)
        return carry

    lax.fori_loop(0, i - lo, body, 0)
    outs = []
    for h in range(nh):
        acc = acc_scr[h]
        den_lane = HEAD_DIM if h % 2 == 0 else 0
        outs.append(acc / acc[:, den_lane:den_lane + 1])
    pairs = [jnp.where(lane < HEAD_DIM, outs[2 * k], outs[2 * k + 1]) for k in range(nh // 2)]
    o_ref[...] = jnp.concatenate(pairs, axis=1).astype(BF16)


def _attn(jlo, qp, ktp, vp, blk, nh, online):
    B, Lp, hw = qp.shape
    n_groups = hw // (nh * LANES)
    gw = nh * LANES
    grid_spec = pltpu.PrefetchScalarGridSpec(
        num_scalar_prefetch=1,
        grid=(B, n_groups, Lp // blk),
        in_specs=[
            pl.BlockSpec((None, blk, gw), lambda b, g, i, jlo_ref: (b, i, g)),
            pl.BlockSpec((None, gw, Lp), lambda b, g, i, jlo_ref: (b, g, 0)),
            pl.BlockSpec((None, Lp, gw), lambda b, g, i, jlo_ref: (b, 0, g)),
        ],
        out_specs=pl.BlockSpec((None, blk, gw // 2), lambda b, g, i, jlo_ref: (b, i, g)),
        scratch_shapes=[pltpu.VMEM((nh, blk, LANES), F32), pltpu.VMEM((nh, blk, 1), F32)],
    )
    return pl.pallas_call(
        functools.partial(_attn_kernel, blk=blk, nh=nh, online=online),
        grid_spec=grid_spec,
        out_shape=jax.ShapeDtypeStruct((B, Lp, hw // 2), BF16),
        compiler_params=_params(("parallel", "parallel", "arbitrary")),
        name="attn_online" if online else "attn",
    )(jlo, qp, ktp, vp)


def _first_live_block(f2, cmax, blk, n_heads, group):
    B, Lp, _ = f2.shape
    nq = Lp // blk
    f_first = f2[:, 0::blk, :n_heads]
    f_last = f2[:, blk - 1::blk, :n_heads]
    gap = f_first[:, :, None, :] - f_last[:, None, :, :]
    jj = jnp.arange(nq)
    live = (gap + 2.0 * cmax >= -SKIP_LOG2) | (jj[None, None, :, None] >= jj[None, :, None, None])
    jlo = jnp.argmax(live, axis=2).astype(jnp.int32)
    jlo = jnp.min(jlo.reshape(B, nq, n_heads // group, group), axis=3)
    return jnp.transpose(jlo, (0, 2, 1)).reshape(-1)


def _outmlp_kernel(h_ref, yrw_ref, yfx_ref, gate_ref, wrw_ref, wfx_ref, wo_ref, gm_ref, wup_ref,
                   wdn_ref, o_ref, *, ff_chunk):
    d = h_ref.shape[1]
    gates = gate_ref[...].astype(F32)
    merged = gates[:, :d] * _dot(yrw_ref[...], wrw_ref[...]) + gates[:, d:] * _dot(yfx_ref[...], wfx_ref[...])
    h1 = h_ref[...] + _dot(merged.astype(BF16), wo_ref[...])
    ms = jnp.mean(h1 * h1, axis=-1, keepdims=True)
    z = (h1 * lax.rsqrt(ms + NORM_EPS) * gm_ref[...]).astype(BF16)
    acc = h1
    for c0 in range(0, wup_ref.shape[1], ff_chunk):
        u = jnp.maximum(_dot(z, wup_ref[:, c0:c0 + ff_chunk]), 0.0)
        acc = acc + _dot((u * u).astype(BF16), wdn_ref[c0:c0 + ff_chunk, :])
    o_ref[...] = acc


def _outmlp(h2, yrw, yfx, gates, wrw, wfx, wo, gm, wup, wdn, tm):
    rows, d = h2.shape
    d_rw, d_fx, d_ff = wrw.shape[0], wfx.shape[0], wup.shape[1]
    return pl.pallas_call(
        functools.partial(_outmlp_kernel, ff_chunk=min(d_ff, 1024)),
        grid=(rows // tm,),
        in_specs=[
            pl.BlockSpec((tm, d), lambda i: (i, 0)),
            pl.BlockSpec((tm, d_rw), lambda i: (i, 0)),
            pl.BlockSpec((tm, d_fx), lambda i: (i, 0)),
            pl.BlockSpec((tm, 2 * d), lambda i: (i, 0)),
            _const_spec((d_rw, d)), _const_spec((d_fx, d)), _const_spec((d, d)),
            _const_spec((1, d)), _const_spec((d, d_ff)), _const_spec((d_ff, d)),
        ],
        out_specs=pl.BlockSpec((tm, d), lambda i: (i, 0)),
        out_shape=jax.ShapeDtypeStruct((rows, d), F32),
        compiler_params=_params(("parallel",)),
        name="outmlp",
    )(h2, yrw, yfx, gates, wrw, wfx, wo, gm, wup, wdn)


def _largest_divisor(n, candidates):
    for c in candidates:
        if n % c == 0:
            return c
    raise ValueError(f"no tile in {candidates} divides {n}")


def _head_pad_cols(w, n_heads, upper_for_odd):
    kdim = w.shape[0]
    w3 = w.reshape(kdim, n_heads, HEAD_DIM)
    z = jnp.zeros_like(w3)
    lower = jnp.concatenate([w3, z], axis=2)
    if not upper_for_odd:
        return lower.reshape(kdim, n_heads * LANES)
    upper = jnp.concatenate([z, w3], axis=2)
    odd = (jnp.arange(n_heads) % 2 == 1)[None, :, None]
    return jnp.where(odd, upper, lower).reshape(kdim, n_heads * LANES)


def _attn_constants(n_heads, shift):
    hw = n_heads * LANES
    sq = jnp.zeros((3 * LANES, hw), F32)
    sk = jnp.zeros((3 * LANES, hw), F32)
    cq = jnp.zeros((1, hw), F32)
    ck = jnp.zeros((1, hw), F32)
    cv = jnp.zeros((1, hw), F32)
    for h in range(n_heads):
        base = h * LANES + HEAD_DIM
        for part in range(3):
            sq = sq.at[part * LANES + h, base + part].set(1.0)
            sk = sk.at[part * LANES + h, base + 3 + part].set(-1.0)
            cq = cq.at[0, base + 3 + part].set(1.0)
            ck = ck.at[0, base + part].set(1.0)
        cq = cq.at[0, base + 6].set(-shift)
        ck = ck.at[0, base + 6].set(1.0)
        cv = cv.at[0, h * LANES + (HEAD_DIM if h % 2 == 0 else 0)].set(1.0)
    return sq.astype(BF16), sk.astype(BF16), cq, ck, cv


def kernel(x, meta, norm_mix, w_in, b_gate, b_f, tm_mu, w0, w_lora_up, a0, a_lora_up, g_lora_up, k_k, k_a,
           r_k, lnx_g, lnx_b, q_gain, k_gain, w_out_rw, w_out_fx, w_o, norm_mlp, w_up, w_down):
    B, S, D = x.shape
    depth = w_in.shape[0]
    d_rw = w0.shape[1]
    d_fx = w_out_fx.shape[1]
    h_fx = b_f.shape[1]
    c_rw = 3 * d_rw + LORA_W + LORA_A + LORA_G
    c_fx = 3 * d_fx + h_fx
    assert d_rw % LANES == 0 and h_fx % 2 == 0 and d_fx == h_fx * HEAD_DIM and h_fx <= LANES
    assert LORA_W + LORA_A == LANES and LORA_G == LANES and HEAD_DIM * 2 == LANES

    L = N_META + S
    Lp = -(-L // SEQ_ALIGN) * SEQ_ALIGN
    rows = B * Lp
    tm_row = _largest_divisor(rows, (256, 128))
    tm_prep = _largest_divisor(Lp, (256, 128))
    blk = SEQ_ALIGN
    hw = h_fx * LANES

    meta_b = jnp.broadcast_to(meta.astype(x.dtype)[None], (B, N_META, D))
    h = jnp.concatenate([meta_b, x, jnp.zeros((B, Lp - L, D), x.dtype)], axis=1).reshape(rows, D)

    seg = jnp.arange(LANES) // HEAD_DIM
    bd = (seg[:, None] == seg[None, :]).astype(BF16)
    zeros_lora = jnp.zeros((LORA_W, d_rw), F32)
    scale = HEAD_DIM ** -0.5
    row = lambda t: t.reshape(1, -1)

    for l in range(depth):
        wl = w_in[l]
        w_rw = wl[:, :c_rw].astype(BF16)
        fx0 = c_rw
        w_fx = jnp.concatenate([
            _head_pad_cols(wl[:, fx0:fx0 + d_fx], h_fx, False),
            _head_pad_cols(wl[:, fx0 + d_fx:fx0 + 2 * d_fx], h_fx, False),
            _head_pad_cols(wl[:, fx0 + 2 * d_fx:fx0 + 3 * d_fx], h_fx, True),
            jnp.pad(wl[:, fx0 + 3 * d_fx:fx0 + c_fx], ((0, 0), (0, LANES - h_fx))),
        ], axis=1).astype(BF16)
        w_g = wl[:, c_rw + c_fx:].astype(BF16)

        feat_rw, gates = _inproj(h, row(norm_mix[l]), w_rw, w_g, row(b_gate[l]), tm_row)

        y_rw = _wkv(
            feat_rw.reshape(B, Lp, c_rw), row(tm_mu[l]), row(w0[l]),
            jnp.concatenate([w_lora_up[l], zeros_lora], axis=0).astype(BF16), row(a0[l]),
            jnp.concatenate([zeros_lora, a_lora_up[l]], axis=0).astype(BF16), g_lora_up[l].astype(BF16),
            row(k_k[l]), row(k_a[l]), row(r_k[l]), row(lnx_g[l]), row(lnx_b[l]), bd, d_rw)

        qg = _head_pad_cols(jnp.tile(q_gain[l] * (scale * LOG2E), h_fx)[None, :], h_fx, False)
        kg = _head_pad_cols(jnp.tile(k_gain[l], h_fx)[None, :], h_fx, False)
        bf = jnp.pad(b_f[l], (0, LANES - h_fx))[None, :]
        cmax = 1.02 * HEAD_DIM * jnp.max(jnp.abs(qg)) * jnp.max(jnp.abs(kg))
        shift = (1.01 * cmax).astype(BF16).astype(F32)
        sq, sk, cq, ck, cv = _attn_constants(h_fx, shift)
        qp, ktp, vp, f2 = _fxprep(h.reshape(B, Lp, D), row(norm_mix[l]), w_fx, qg, kg, bf, sq, sk, cq, ck, cv,
                                  h_fx, tm_prep)
        jlo = _first_live_block(f2, cmax, blk, h_fx, ATTN_GROUP)
        y_fx = lax.cond(
            cmax <= FIXED_SHIFT_MAX_LOG2,
            lambda ops: _attn(*ops, blk, ATTN_GROUP, False),
            lambda ops: _attn(*ops, blk, ATTN_GROUP, True),
            (jlo, qp, ktp, vp))

        h = _outmlp(h, y_rw.reshape(rows, d_rw), y_fx.reshape(rows, d_fx), gates,
                    w_out_rw[l].astype(BF16), w_out_fx[l].astype(BF16), w_o[l].astype(BF16),
                    row(norm_mlp[l]), w_up[l].astype(BF16), w_down[l].astype(BF16), tm_row)

    return h.reshape(B, Lp, D)[:, N_META:N_META + S]
```

```python
import functools

import jax
import jax.numpy as jnp
import numpy as np
from jax import lax
from jax.experimental import pallas as pl
from jax.experimental.pallas import tpu as pltpu

F32 = jnp.float32
BF16 = jnp.bfloat16

N_META = 16
HEAD_DIM = 64
LORA_W = 64
LORA_A = 64
LORA_G = 128
NORM_EPS = 1e-6
GN_EPS = 64e-5
DECAY_OFFSET = 0.5
LOG2E = 1.4426950408889634

LANES = 128
SEQ_ALIGN = 256
SKIP_LOG2 = 160.0
WKV_CHUNK = 128
WKV_BATCH_ROWS = 2
ATTN_GROUP = 4
FIXED_SHIFT_MAX_LOG2 = 50.0
VMEM_LIMIT_BYTES = 56 * 1024 * 1024
NEG_BIG = -1e30


def _dot(a, b):
    return jnp.dot(a, b, preferred_element_type=F32)


def _dot_nt(a, b):
    return lax.dot_general(a, b, (((1,), (1,)), ((), ())), preferred_element_type=F32)


def _split2(x):
    hi = x.astype(BF16)
    lo = (x - hi.astype(F32)).astype(BF16)
    return hi, lo


def _split3(x):
    hi = x.astype(BF16)
    r1 = x - hi.astype(F32)
    mid = r1.astype(BF16)
    lo = (r1 - mid.astype(F32)).astype(BF16)
    return hi, mid, lo


def _dot_exact_lhs(m_bf16, x):
    hi, mid, lo = _split3(x)
    return _dot(m_bf16, hi) + _dot(m_bf16, mid) + _dot(m_bf16, lo)


def _sigmoid(x):
    return 1.0 / (1.0 + jnp.exp(-x))


def _softplus(x):
    return jnp.maximum(x, 0.0) + jnp.log(1.0 + jnp.exp(-jnp.abs(x)))


def _const_spec(shape):
    nd = len(shape)
    return pl.BlockSpec(shape, lambda *_: (0,) * nd)


def _params(sem):
    return pltpu.CompilerParams(dimension_semantics=sem, vmem_limit_bytes=VMEM_LIMIT_BYTES)


def _rmsnorm_bf16(x, g):
    ms = jnp.mean(x * x, axis=-1, keepdims=True)
    return (x * lax.rsqrt(ms + NORM_EPS) * g).astype(BF16)


def _inproj_kernel(h_ref, g_ref, wrw_ref, wg_ref, bg_ref, rw_ref, gate_ref):
    u = _rmsnorm_bf16(h_ref[...], g_ref[...])
    rw_ref[...] = _dot(u, wrw_ref[...])
    gate_ref[...] = _sigmoid(_dot(u, wg_ref[...]) + bg_ref[...]).astype(BF16)


def _inproj(h2, g, wrw, wg, bg, tm):
    rows, d = h2.shape
    n_rw, n_g = wrw.shape[1], wg.shape[1]
    return pl.pallas_call(
        _inproj_kernel,
        grid=(rows // tm,),
        in_specs=[
            pl.BlockSpec((tm, d), lambda i: (i, 0)),
            _const_spec((1, d)),
            _const_spec((d, n_rw)),
            _const_spec((d, n_g)),
            _const_spec((1, n_g)),
        ],
        out_specs=[
            pl.BlockSpec((tm, n_rw), lambda i: (i, 0)),
            pl.BlockSpec((tm, n_g), lambda i: (i, 0)),
        ],
        out_shape=[
            jax.ShapeDtypeStruct((rows, n_rw), F32),
            jax.ShapeDtypeStruct((rows, n_g), BF16),
        ],
        compiler_params=_params(("parallel",)),
        name="inproj",
    )(h2, g, wrw, wg, bg)


def _wkv_kernel(x_ref, mu_ref, w0_ref, wlw_ref, a0_ref, wla_ref, wlg_ref, kk_ref, ka_ref, rk_ref,
                lng_ref, lnb_ref, bd_ref, o_ref, m_ref, prev_ref, *, d_rw, bb):
    T = WKV_CHUNK
    c = pl.program_id(1)
    n_pairs = d_rw // LANES
    c_rw = x_ref.shape[2]

    @pl.when(c == 0)
    def _():
        m_ref[...] = jnp.zeros_like(m_ref)
        prev_ref[...] = jnp.zeros_like(prev_ref)

    x = x_ref[...].reshape(bb * T, c_rw)
    row = lax.broadcasted_iota(jnp.int32, x.shape, 0)
    shifted = pltpu.roll(x, 1, 0)
    for bi in range(bb):
        shifted = jnp.where(row == bi * T, prev_ref[bi], shifted)
        prev_ref[bi] = x[(bi + 1) * T - 1:(bi + 1) * T, :]
    f = x + mu_ref[...] * (shifted - x)

    r = f[:, 0:d_rw]
    k = f[:, d_rw:2 * d_rw]
    v = f[:, 2 * d_rw:3 * d_rw]
    wa_d = f[:, 3 * d_rw:3 * d_rw + LORA_W + LORA_A]
    g_d = f[:, 3 * d_rw + LORA_W + LORA_A:]

    bd = bd_ref[...]

    def segsum(t):
        stacked = jnp.concatenate([t[:, p * LANES:(p + 1) * LANES] for p in range(n_pairs)], axis=0)
        out = _dot(stacked.astype(BF16), bd)
        rows = t.shape[0]
        return jnp.concatenate([out[p * rows:(p + 1) * rows] for p in range(n_pairs)], axis=1)

    w_log = -_softplus(-(w0_ref[...] + _dot(jnp.tanh(wa_d).astype(BF16), wlw_ref[...]))) - DECAY_OFFSET
    lw = -jnp.exp(w_log)
    a_lr = _sigmoid(a0_ref[...] + _dot(wa_d.astype(BF16), wla_ref[...]))
    g = _dot(_sigmoid(g_d).astype(BF16), wlg_ref[...])
    kk = k * kk_ref[...]
    k2 = k * (1.0 + (a_lr - 1.0) * ka_ref[...])
    sums = segsum(jnp.concatenate([kk * kk, r * k2 * rk_ref[...]], axis=0))
    kk = kk / jnp.maximum(jnp.sqrt(sums[:bb * T]), 1e-12)
    bonus = sums[bb * T:] * v
    a_vec = -kk
    b_vec = kk * a_lr

    ri = lax.broadcasted_iota(jnp.int32, (T, T), 0)
    ci = lax.broadcasted_iota(jnp.int32, (T, T), 1)
    strict = ci < ri
    incl = ci <= ri
    eye = ci == ri
    tri_incl_bf = incl.astype(BF16)
    rc_xor = jnp.bitwise_xor(ri, ci)
    level_masks = []
    n = 1
    while n < T:
        level_masks.append(strict & (rc_xor >= n) & (rc_xor < 2 * n))
        n *= 2
    lane = lax.broadcasted_iota(jnp.int32, (T, LANES), 1)
    head0 = lane < HEAD_DIM
    pr = lax.broadcasted_iota(jnp.int32, (LANES, LANES), 0)
    pc = lax.broadcasted_iota(jnp.int32, (LANES, LANES), 1)
    same_head = (pr < HEAD_DIM) == (pc < HEAD_DIM)
    eye_p = pr == pc

    units = [(bi, p) for bi in range(bb) for p in range(n_pairs)]
    tile = lambda t, u: t[units[u][0] * T:(units[u][0] + 1) * T, units[u][1] * LANES:(units[u][1] + 1) * LANES]
    nu = range(len(units))
    heads = [(u, hh) for u in nu for hh in range(2)]
    zero = jnp.zeros((T, LANES), F32)

    lw_hi, lw_lo = _split2(lw)
    rt, vb, atb, dend_t, dt_row, sc = [], [], [], [], [], []
    for u in nu:
        lwu = tile(lw, u)
        clu = _dot(tri_incl_bf, tile(lw_hi, u)) + _dot(tri_incl_bf, tile(lw_lo, u))
        cl_last = clu[T - 1:T, :]
        dinv = jnp.exp(-clu)
        dend = jnp.exp(cl_last - clu)
        at = tile(a_vec, u) * jnp.exp(clu - lwu)
        bt = tile(b_vec, u) * dinv
        kt = tile(k2, u) * dinv
        rtu = tile(r, u) * jnp.exp(clu)
        rt.append(rtu)
        vb.append(tile(v, u).astype(BF16))
        atb.append(at.astype(BF16))
        dend_t.append(jnp.concatenate([(tile(b_vec, u) * dend).T, (tile(k2, u) * dend).T], axis=1).astype(BF16))
        dt_row.append(jnp.exp(cl_last))
        lhs = jnp.concatenate([jnp.where(head0, at, zero), jnp.where(head0, zero, at),
                               jnp.where(head0, rtu, zero), jnp.where(head0, zero, rtu)], axis=0).astype(BF16)
        rhs = jnp.concatenate([bt, kt], axis=0).astype(BF16)
        sc.append(_dot_nt(lhs, rhs))

    aab, aak, arbk = {}, {}, {}
    for (u, hh) in heads:
        aab[u, hh] = jnp.where(strict, sc[u][hh * T:(hh + 1) * T, 0:T], 0.0)
        aak[u, hh] = jnp.where(strict, sc[u][hh * T:(hh + 1) * T, T:2 * T], 0.0).astype(BF16)
        arbk[u, hh] = jnp.where(jnp.concatenate([incl, incl], axis=1), sc[u][(2 + hh) * T:(3 + hh) * T, :],
                                0.0).astype(BF16)

    tinv = {hd: jnp.where(eye, 1.0, jnp.where(level_masks[0], aab[hd], 0.0)).astype(BF16) for hd in heads}
    for lm in level_masks[1:]:
        half = {hd: _dot(tinv[hd], jnp.where(lm, aab[hd], 0.0).astype(BF16)).astype(BF16) for hd in heads}
        tinv = {hd: (tinv[hd].astype(F32) + _dot(half[hd], tinv[hd])).astype(BF16) for hd in heads}

    aakv = {hd: _dot(aak[hd], vb[hd[0]]).astype(BF16) for hd in heads}
    ap_h = {hd: _dot(tinv[hd], atb[hd[0]]) for hd in heads}
    w2_h = {hd: _dot(tinv[hd], aakv[hd]) for hd in heads}
    apn = [jnp.where(head0, ap_h[u, 0], ap_h[u, 1]).astype(BF16) for u in nu]
    w2v = [jnp.concatenate([jnp.where(head0, w2_h[u, 0], w2_h[u, 1]).astype(BF16), vb[u]], axis=0) for u in nu]
    qp_h = {hd: _dot(arbk[hd][:, 0:T], apn[hd[0]]) for hd in heads}
    yl_h = {hd: _dot(arbk[hd], w2v[hd[0]]) for hd in heads}

    ys = []
    for u in nu:
        qp = rt[u] + jnp.where(head0, qp_h[u, 0], qp_h[u, 1])
        yl = jnp.where(head0, yl_h[u, 0], yl_h[u, 1])
        g_mat = jnp.where(same_head, _dot(dend_t[u][:, 0:T], apn[u]), 0.0) + jnp.where(eye_p, dt_row[u], 0.0)
        c_mat = jnp.where(same_head, _dot(dend_t[u], w2v[u]), 0.0)
        m0 = m_ref[u].astype(BF16)
        ys.append(_dot(qp.astype(BF16), m0) + yl)
        m_ref[u] = _dot(g_mat.astype(BF16), m0) + c_mat

    y = jnp.concatenate([jnp.concatenate(ys[bi * n_pairs:(bi + 1) * n_pairs], axis=1) for bi in range(bb)],
                        axis=0)
    inv_n = 1.0 / HEAD_DIM
    mean = segsum(y) * inv_n
    yc = y - mean
    var = segsum(yc * yc) * inv_n
    yn = yc * lax.rsqrt(var + GN_EPS) * lng_ref[...] + lnb_ref[...]
    o_ref[...] = ((yn + bonus) * g).astype(BF16).reshape(bb, T, d_rw)


def _wkv(feat_rw, mu, w0, wlw, a0, wla, wlg, k_k, k_a, r_k, lnx_g, lnx_b, bd, d_rw, bb):
    B, Lp, c_rw = feat_rw.shape
    T = WKV_CHUNK
    vec = _const_spec((1, d_rw))
    return pl.pallas_call(
        functools.partial(_wkv_kernel, d_rw=d_rw, bb=bb),
        grid=(B // bb, Lp // T),
        in_specs=[
            pl.BlockSpec((bb, T, c_rw), lambda b, c: (b, c, 0)),
            _const_spec((1, c_rw)),
            vec, _const_spec((LANES, d_rw)), vec, _const_spec((LANES, d_rw)), _const_spec((LORA_G, d_rw)),
            vec, vec, vec, vec, vec,
            _const_spec((LANES, LANES)),
        ],
        out_specs=pl.BlockSpec((bb, T, d_rw), lambda b, c: (b, c, 0)),
        out_shape=jax.ShapeDtypeStruct((B, Lp, d_rw), BF16),
        scratch_shapes=[pltpu.VMEM((bb * (d_rw // LANES), LANES, LANES), F32),
                        pltpu.VMEM((bb, 1, c_rw), F32)],
        compiler_params=_params(("parallel", "arbitrary")),
        name="wkv",
    )(feat_rw, mu, w0, wlw, a0, wla, wlg, k_k, k_a, r_k, lnx_g, lnx_b, bd)


def _fxprep_kernel(h_ref, g_ref, wfx_ref, qg_ref, kg_ref, bf_ref, sq_ref, sk_ref, cq_ref, ck_ref, cv_ref,
                   q_ref, kt_ref, v_ref, f2_ref, carry_ref, *, n_heads):
    i = pl.program_id(1)

    @pl.when(i == 0)
    def _():
        carry_ref[...] = jnp.zeros_like(carry_ref)

    hw = n_heads * LANES
    x = _dot(_rmsnorm_bf16(h_ref[...], g_ref[...]), wfx_ref[...])
    tm = x.shape[0]
    fl = x[:, 3 * hw:3 * hw + LANES]
    z = fl + bf_ref[...]
    log_f = jnp.minimum(z, 0.0) - jnp.log(1.0 + jnp.exp(-jnp.abs(z)))
    lane = lax.broadcasted_iota(jnp.int32, (tm, LANES), 1)
    log_f = jnp.where(lane < n_heads, log_f, 0.0)
    ri = lax.broadcasted_iota(jnp.int32, (tm, tm), 0)
    ci = lax.broadcasted_iota(jnp.int32, (tm, tm), 1)
    tri = (ci <= ri).astype(BF16)
    fcum = carry_ref[...] + _dot_exact_lhs(tri, log_f)
    carry_ref[...] = fcum[tm - 1:tm, :]
    f2 = fcum * LOG2E
    f2_ref[...] = f2
    hi, mid, lo = _split3(f2)
    parts = (hi.astype(F32) + pltpu.roll(mid.astype(F32), n_heads, 1)
             + pltpu.roll(lo.astype(F32), 2 * n_heads, 1)).astype(BF16)
    extra_q = _dot(parts, sq_ref[...]) + cq_ref[...]
    extra_k = _dot(parts, sk_ref[...]) + ck_ref[...]

    inv_n = 1.0 / HEAD_DIM
    for h in range(n_heads):
        sl = slice(h * LANES, (h + 1) * LANES)
        qh = x[:, sl]
        kh = x[:, hw + h * LANES: hw + (h + 1) * LANES]
        qn = qh * lax.rsqrt(jnp.sum(qh * qh, axis=-1, keepdims=True) * inv_n + NORM_EPS) * qg_ref[:, sl]
        kn = kh * lax.rsqrt(jnp.sum(kh * kh, axis=-1, keepdims=True) * inv_n + NORM_EPS) * kg_ref[:, sl]
        q_ref[:, sl] = (qn + extra_q[:, sl]).astype(BF16)
        kt_ref[sl, :] = (kn + extra_k[:, sl]).T.astype(BF16)
    v_ref[...] = (x[:, 2 * hw:3 * hw] + cv_ref[...]).astype(BF16)


def _fxprep(h3, g, wfx, qg, kg, bf, sq, sk, cq, ck, cv, n_heads, tm):
    B, Lp, d = h3.shape
    n_fx = wfx.shape[1]
    hw = n_heads * LANES
    return pl.pallas_call(
        functools.partial(_fxprep_kernel, n_heads=n_heads),
        grid=(B, Lp // tm),
        in_specs=[
            pl.BlockSpec((None, tm, d), lambda b, i: (b, i, 0)),
            _const_spec((1, d)), _const_spec((d, n_fx)),
            _const_spec((1, hw)), _const_spec((1, hw)), _const_spec((1, LANES)),
            _const_spec((LANES, hw)), _const_spec((LANES, hw)),
            _const_spec((1, hw)), _const_spec((1, hw)), _const_spec((1, hw)),
        ],
        out_specs=[
            pl.BlockSpec((None, tm, hw), lambda b, i: (b, i, 0)),
            pl.BlockSpec((None, hw, tm), lambda b, i: (b, 0, i)),
            pl.BlockSpec((None, tm, hw), lambda b, i: (b, i, 0)),
            pl.BlockSpec((None, tm, LANES), lambda b, i: (b, i, 0)),
        ],
        out_shape=[
            jax.ShapeDtypeStruct((B, Lp, hw), BF16),
            jax.ShapeDtypeStruct((B, hw, Lp), BF16),
            jax.ShapeDtypeStruct((B, Lp, hw), BF16),
            jax.ShapeDtypeStruct((B, Lp, LANES), F32),
        ],
        scratch_shapes=[pltpu.VMEM((1, LANES), F32)],
        compiler_params=_params(("parallel", "arbitrary")),
        name="fxprep",
    )(h3, g, wfx, qg, kg, bf, sq, sk, cq, ck, cv)


def _attn_kernel(jlo_ref, q_ref, kt_ref, v_ref, o_ref, acc_scr, m_scr, *, blk, nh, online):
    b, g, i = pl.program_id(0), pl.program_id(1), pl.program_id(2)
    lo = jlo_ref[(b * pl.num_programs(1) + g) * pl.num_programs(2) + i]
    ri = lax.broadcasted_iota(jnp.int32, (blk, blk), 0)
    ci = lax.broadcasted_iota(jnp.int32, (blk, blk), 1)
    causal = ci <= ri
    lane = lax.broadcasted_iota(jnp.int32, (blk, LANES), 1)
    hs = [slice(h * LANES, (h + 1) * LANES) for h in range(nh)]

    def block(j, diagonal):
        off = pl.multiple_of(j * blk, blk)
        s = [_dot(q_ref[:, hs[h]], kt_ref[hs[h], pl.ds(off, blk)]) for h in range(nh)]
        probs, alphas = [], []
        for h in range(nh):
            sm = jnp.where(causal, s[h], NEG_BIG) if diagonal else s[h]
            if online:
                row_max = jnp.max(sm, axis=-1, keepdims=True)
                if diagonal:
                    m_new = row_max
                else:
                    m_old = m_scr[h]
                    m_new = jnp.maximum(m_old, row_max)
                    alphas.append(jnp.exp2(m_old - m_new))
                m_scr[h] = m_new
                sm = sm - m_new
            probs.append(jnp.exp2(sm).astype(BF16))
        for h in range(nh):
            pv = _dot(probs[h], v_ref[pl.ds(off, blk), hs[h]])
            if diagonal:
                acc_scr[h] = pv
            elif online:
                acc_scr[h] = alphas[h] * acc_scr[h] + pv
            else:
                acc_scr[h] += pv

    block(i, True)

    def body(n, carry):
        block(i - 1 - n, False)
        return carry

    lax.fori_loop(0, i - lo, body, 0)
    pairs = []
    for k in range(nh // 2):
        even, odd = acc_scr[2 * k], acc_scr[2 * k + 1]
        num = jnp.where(lane < HEAD_DIM, even, odd)
        den = pltpu.roll(jnp.where(lane < HEAD_DIM, odd, even), HEAD_DIM, 1)
        pairs.append(num / den)
    o_ref[...] = jnp.concatenate(pairs, axis=1).astype(BF16)


def _attn(jlo, qp, ktp, vp, blk, nh, online):
    B, Lp, hw = qp.shape
    n_groups = hw // (nh * LANES)
    gw = nh * LANES
    grid_spec = pltpu.PrefetchScalarGridSpec(
        num_scalar_prefetch=1,
        grid=(B, n_groups, Lp // blk),
        in_specs=[
            pl.BlockSpec((None, blk, gw), lambda b, g, i, jlo_ref: (b, i, g)),
            pl.BlockSpec((None, gw, Lp), lambda b, g, i, jlo_ref: (b, g, 0)),
            pl.BlockSpec((None, Lp, gw), lambda b, g, i, jlo_ref: (b, 0, g)),
        ],
        out_specs=pl.BlockSpec((None, blk, gw // 2), lambda b, g, i, jlo_ref: (b, i, g)),
        scratch_shapes=[pltpu.VMEM((nh, blk, LANES), F32), pltpu.VMEM((nh, blk, 1), F32)],
    )
    return pl.pallas_call(
        functools.partial(_attn_kernel, blk=blk, nh=nh, online=online),
        grid_spec=grid_spec,
        out_shape=jax.ShapeDtypeStruct((B, Lp, hw // 2), BF16),
        compiler_params=_params(("parallel", "parallel", "arbitrary")),
        name="attn_online" if online else "attn",
    )(jlo, qp, ktp, vp)


def _first_live_block(f2, cmax, blk, n_heads, group):
    B, Lp, _ = f2.shape
    nq = Lp // blk
    f_first = f2[:, 0::blk, :n_heads]
    f_last = f2[:, blk - 1::blk, :n_heads]
    gap = f_first[:, :, None, :] - f_last[:, None, :, :]
    jj = jnp.arange(nq)
    live = (gap + 2.0 * cmax >= -SKIP_LOG2) | (jj[None, None, :, None] >= jj[None, :, None, None])
    jlo = jnp.argmax(live, axis=2).astype(jnp.int32)
    jlo = jnp.min(jlo.reshape(B, nq, n_heads // group, group), axis=3)
    return jnp.transpose(jlo, (0, 2, 1)).reshape(-1)


def _outmlp_kernel(h_ref, yrw_ref, yfx_ref, gate_ref, wrw_ref, wfx_ref, wo_ref, gm_ref, wup_ref,
                   wdn_ref, o_ref, *, ff_chunk):
    d = h_ref.shape[1]
    gates = gate_ref[...].astype(F32)
    merged = gates[:, :d] * _dot(yrw_ref[...], wrw_ref[...]) + gates[:, d:] * _dot(yfx_ref[...], wfx_ref[...])
    h1 = h_ref[...] + _dot(merged.astype(BF16), wo_ref[...])
    ms = jnp.mean(h1 * h1, axis=-1, keepdims=True)
    z = (h1 * lax.rsqrt(ms + NORM_EPS) * gm_ref[...]).astype(BF16)
    acc = h1
    for c0 in range(0, wup_ref.shape[1], ff_chunk):
        u = jnp.maximum(_dot(z, wup_ref[:, c0:c0 + ff_chunk]), 0.0)
        acc = acc + _dot((u * u).astype(BF16), wdn_ref[c0:c0 + ff_chunk, :])
    o_ref[...] = acc


def _outmlp(h2, yrw, yfx, gates, wrw, wfx, wo, gm, wup, wdn, tm):
    rows, d = h2.shape
    d_rw, d_fx, d_ff = wrw.shape[0], wfx.shape[0], wup.shape[1]
    return pl.pallas_call(
        functools.partial(_outmlp_kernel, ff_chunk=min(d_ff, 1024)),
        grid=(rows // tm,),
        in_specs=[
            pl.BlockSpec((tm, d), lambda i: (i, 0)),
            pl.BlockSpec((tm, d_rw), lambda i: (i, 0)),
            pl.BlockSpec((tm, d_fx), lambda i: (i, 0)),
            pl.BlockSpec((tm, 2 * d), lambda i: (i, 0)),
            _const_spec((d_rw, d)), _const_spec((d_fx, d)), _const_spec((d, d)),
            _const_spec((1, d)), _const_spec((d, d_ff)), _const_spec((d_ff, d)),
        ],
        out_specs=pl.BlockSpec((tm, d), lambda i: (i, 0)),
        out_shape=jax.ShapeDtypeStruct((rows, d), F32),
        compiler_params=_params(("parallel",)),
        name="outmlp",
    )(h2, yrw, yfx, gates, wrw, wfx, wo, gm, wup, wdn)


def _largest_divisor(n, candidates):
    for c in candidates:
        if n % c == 0:
            return c
    raise ValueError(f"no tile in {candidates} divides {n}")


def _head_pad_cols(w, n_heads, upper_for_odd):
    kdim = w.shape[0]
    w3 = w.reshape(kdim, n_heads, HEAD_DIM)
    z = jnp.zeros_like(w3)
    lower = jnp.concatenate([w3, z], axis=2)
    if not upper_for_odd:
        return lower.reshape(kdim, n_heads * LANES)
    upper = jnp.concatenate([z, w3], axis=2)
    odd = (jnp.arange(n_heads) % 2 == 1)[None, :, None]
    return jnp.where(odd, upper, lower).reshape(kdim, n_heads * LANES)


def _attn_constants(n_heads, shift):
    hw = n_heads * LANES
    sq = np.zeros((LANES, hw), np.float32)
    sk = np.zeros((LANES, hw), np.float32)
    cq = np.zeros((1, hw), np.float32)
    cq_shift = np.zeros((1, hw), np.float32)
    ck = np.zeros((1, hw), np.float32)
    cv = np.zeros((1, hw), np.float32)
    for h in range(n_heads):
        base = h * LANES + HEAD_DIM
        for part in range(3):
            sq[part * n_heads + h, base + part] = 1.0
            sk[part * n_heads + h, base + 3 + part] = -1.0
            cq[0, base + 3 + part] = 1.0
            ck[0, base + part] = 1.0
        cq_shift[0, base + 6] = -1.0
        ck[0, base + 6] = 1.0
        ones0 = h * LANES + (HEAD_DIM if h % 2 == 0 else 0)
        cv[0, ones0:ones0 + HEAD_DIM] = 1.0
    return (jnp.asarray(sq, BF16), jnp.asarray(sk, BF16), jnp.asarray(cq) + shift * jnp.asarray(cq_shift),
            jnp.asarray(ck), jnp.asarray(cv))


def kernel(x, meta, norm_mix, w_in, b_gate, b_f, tm_mu, w0, w_lora_up, a0, a_lora_up, g_lora_up, k_k, k_a,
           r_k, lnx_g, lnx_b, q_gain, k_gain, w_out_rw, w_out_fx, w_o, norm_mlp, w_up, w_down):
    B, S, D = x.shape
    depth = w_in.shape[0]
    d_rw = w0.shape[1]
    d_fx = w_out_fx.shape[1]
    h_fx = b_f.shape[1]
    c_rw = 3 * d_rw + LORA_W + LORA_A + LORA_G
    c_fx = 3 * d_fx + h_fx
    assert d_rw % LANES == 0 and h_fx % 2 == 0 and d_fx == h_fx * HEAD_DIM and 3 * h_fx <= LANES
    assert LORA_W + LORA_A == LANES and LORA_G == LANES and HEAD_DIM * 2 == LANES

    L = N_META + S
    Lp = -(-L // SEQ_ALIGN) * SEQ_ALIGN
    rows = B * Lp
    tm_row = _largest_divisor(rows, (256, 128))
    tm_prep = _largest_divisor(Lp, (256, 128))
    blk = SEQ_ALIGN
    wkv_bb = _largest_divisor(B, (WKV_BATCH_ROWS, 1))
    hw = h_fx * LANES

    meta_b = jnp.broadcast_to(meta.astype(x.dtype)[None], (B, N_META, D))
    h = jnp.concatenate([meta_b, x, jnp.zeros((B, Lp - L, D), x.dtype)], axis=1).reshape(rows, D)

    seg = jnp.arange(LANES) // HEAD_DIM
    bd = (seg[:, None] == seg[None, :]).astype(BF16)
    zeros_lora = jnp.zeros((LORA_W, d_rw), F32)
    scale = HEAD_DIM ** -0.5
    row = lambda t: t.reshape(1, -1)

    for l in range(depth):
        wl = w_in[l]
        w_rw = wl[:, :c_rw].astype(BF16)
        fx0 = c_rw
        w_fx = jnp.concatenate([
            _head_pad_cols(wl[:, fx0:fx0 + d_fx], h_fx, False),
            _head_pad_cols(wl[:, fx0 + d_fx:fx0 + 2 * d_fx], h_fx, False),
            _head_pad_cols(wl[:, fx0 + 2 * d_fx:fx0 + 3 * d_fx], h_fx, True),
            jnp.pad(wl[:, fx0 + 3 * d_fx:fx0 + c_fx], ((0, 0), (0, LANES - h_fx))),
        ], axis=1).astype(BF16)
        w_g = wl[:, c_rw + c_fx:].astype(BF16)

        feat_rw, gates = _inproj(h, row(norm_mix[l]), w_rw, w_g, row(b_gate[l]), tm_row)

        y_rw = _wkv(
            feat_rw.reshape(B, Lp, c_rw), row(tm_mu[l]), row(w0[l]),
            jnp.concatenate([w_lora_up[l], zeros_lora], axis=0).astype(BF16), row(a0[l]),
            jnp.concatenate([zeros_lora, a_lora_up[l]], axis=0).astype(BF16), g_lora_up[l].astype(BF16),
            row(k_k[l]), row(k_a[l]), row(r_k[l]), row(lnx_g[l]), row(lnx_b[l]), bd, d_rw, wkv_bb)

        qg = _head_pad_cols(jnp.tile(q_gain[l] * (scale * LOG2E), h_fx)[None, :], h_fx, False)
        kg = _head_pad_cols(jnp.tile(k_gain[l], h_fx)[None, :], h_fx, False)
        bf = jnp.pad(b_f[l], (0, LANES - h_fx))[None, :]
        cmax = 1.02 * HEAD_DIM * jnp.max(jnp.abs(qg)) * jnp.max(jnp.abs(kg))
        shift = (1.01 * cmax).astype(BF16).astype(F32)
        sq, sk, cq, ck, cv = _attn_constants(h_fx, shift)
        qp, ktp, vp, f2 = _fxprep(h.reshape(B, Lp, D), row(norm_mix[l]), w_fx, qg, kg, bf, sq, sk, cq, ck, cv,
                                  h_fx, tm_prep)
        jlo = _first_live_block(f2, cmax, blk, h_fx, ATTN_GROUP)
        y_fx = lax.cond(
            cmax <= FIXED_SHIFT_MAX_LOG2,
            lambda ops: _attn(*ops, blk, ATTN_GROUP, False),
            lambda ops: _attn(*ops, blk, ATTN_GROUP, True),
            (jlo, qp, ktp, vp))

        h = _outmlp(h, y_rw.reshape(rows, d_rw), y_fx.reshape(rows, d_fx), gates,
                    w_out_rw[l].astype(BF16), w_out_fx[l].astype(BF16), w_o[l].astype(BF16),
                    row(norm_mlp[l]), w_up[l].astype(BF16), w_down[l].astype(BF16), tm_row)

    return h.reshape(B, Lp, D)[:, N_META:N_META + S]
```

```python
import functools

import jax
import jax.numpy as jnp
import numpy as np
from jax import lax
from jax.experimental import pallas as pl
from jax.experimental.pallas import tpu as pltpu

F32 = jnp.float32
BF16 = jnp.bfloat16

N_META = 16
HEAD_DIM = 64
LORA_W = 64
LORA_A = 64
LORA_G = 128
NORM_EPS = 1e-6
GN_EPS = 64e-5
DECAY_OFFSET = 0.5
LOG2E = 1.4426950408889634

LANES = 128
SEQ_ALIGN = 256
SKIP_LOG2 = 160.0
WKV_CHUNK = 128
WKV_BATCH_ROWS = 2
ATTN_GROUP = 4
FIXED_SHIFT_MAX_LOG2 = 50.0
VMEM_LIMIT_BYTES = 56 * 1024 * 1024
NEG_BIG = -1e30


def _dot(a, b):
    return jnp.dot(a, b, preferred_element_type=F32)


def _dot_nt(a, b):
    return lax.dot_general(a, b, (((1,), (1,)), ((), ())), preferred_element_type=F32)


def _split2(x):
    hi = x.astype(BF16)
    lo = (x - hi.astype(F32)).astype(BF16)
    return hi, lo


def _split3(x):
    hi = x.astype(BF16)
    r1 = x - hi.astype(F32)
    mid = r1.astype(BF16)
    lo = (r1 - mid.astype(F32)).astype(BF16)
    return hi, mid, lo


def _dot_exact_lhs(m_bf16, x):
    hi, mid, lo = _split3(x)
    return _dot(m_bf16, hi) + _dot(m_bf16, mid) + _dot(m_bf16, lo)


def _sigmoid(x):
    return 1.0 / (1.0 + jnp.exp(-x))


def _softplus(x):
    return jnp.maximum(x, 0.0) + jnp.log(1.0 + jnp.exp(-jnp.abs(x)))


def _const_spec(shape):
    nd = len(shape)
    return pl.BlockSpec(shape, lambda *_: (0,) * nd, pipeline_mode=pl.Buffered(1))


def _params(sem):
    return pltpu.CompilerParams(dimension_semantics=sem, vmem_limit_bytes=VMEM_LIMIT_BYTES)


def _rmsnorm_bf16(x, g):
    ms = jnp.mean(x * x, axis=-1, keepdims=True)
    return (x * lax.rsqrt(ms + NORM_EPS) * g).astype(BF16)


def _inproj_kernel(h_ref, g_ref, wrw_ref, wg_ref, bg_ref, rw_ref, gate_ref):
    u = _rmsnorm_bf16(h_ref[...], g_ref[...])
    rw_ref[...] = _dot(u, wrw_ref[...])
    gate_ref[...] = _sigmoid(_dot(u, wg_ref[...]) + bg_ref[...]).astype(BF16)


def _inproj(h2, g, wrw, wg, bg, tm):
    rows, d = h2.shape
    n_rw, n_g = wrw.shape[1], wg.shape[1]
    return pl.pallas_call(
        _inproj_kernel,
        grid=(rows // tm,),
        in_specs=[
            pl.BlockSpec((tm, d), lambda i: (i, 0)),
            _const_spec((1, d)),
            _const_spec((d, n_rw)),
            _const_spec((d, n_g)),
            _const_spec((1, n_g)),
        ],
        out_specs=[
            pl.BlockSpec((tm, n_rw), lambda i: (i, 0)),
            pl.BlockSpec((tm, n_g), lambda i: (i, 0)),
        ],
        out_shape=[
            jax.ShapeDtypeStruct((rows, n_rw), F32),
            jax.ShapeDtypeStruct((rows, n_g), BF16),
        ],
        compiler_params=_params(("parallel",)),
        name="inproj",
    )(h2, g, wrw, wg, bg)


def _wkv_kernel(x_ref, mu_ref, w0_ref, wlw_ref, a0_ref, wla_ref, wlg_ref, kk_ref, ka_ref, rk_ref,
                lng_ref, lnb_ref, bd_ref, o_ref, m_ref, prev_ref, *, d_rw, bb):
    T = WKV_CHUNK
    c = pl.program_id(1)
    n_pairs = d_rw // LANES
    c_rw = x_ref.shape[2]

    @pl.when(c == 0)
    def _():
        m_ref[...] = jnp.zeros_like(m_ref)
        prev_ref[...] = jnp.zeros_like(prev_ref)

    x = x_ref[...].reshape(bb * T, c_rw)
    row = lax.broadcasted_iota(jnp.int32, x.shape, 0)
    shifted = pltpu.roll(x, 1, 0)
    for bi in range(bb):
        shifted = jnp.where(row == bi * T, prev_ref[bi], shifted)
        prev_ref[bi] = x[(bi + 1) * T - 1:(bi + 1) * T, :]
    f = x + mu_ref[...] * (shifted - x)

    r = f[:, 0:d_rw]
    k = f[:, d_rw:2 * d_rw]
    v = f[:, 2 * d_rw:3 * d_rw]
    wa_d = f[:, 3 * d_rw:3 * d_rw + LORA_W + LORA_A]
    g_d = f[:, 3 * d_rw + LORA_W + LORA_A:]

    bd = bd_ref[...]

    def segsum(t):
        stacked = jnp.concatenate([t[:, p * LANES:(p + 1) * LANES] for p in range(n_pairs)], axis=0)
        out = _dot(stacked.astype(BF16), bd)
        rows = t.shape[0]
        return jnp.concatenate([out[p * rows:(p + 1) * rows] for p in range(n_pairs)], axis=1)

    w_log = -_softplus(-(w0_ref[...] + _dot(jnp.tanh(wa_d).astype(BF16), wlw_ref[...]))) - DECAY_OFFSET
    lw = -jnp.exp(w_log)
    a_lr = _sigmoid(a0_ref[...] + _dot(wa_d.astype(BF16), wla_ref[...]))
    g = _dot(_sigmoid(g_d).astype(BF16), wlg_ref[...])
    kk = k * kk_ref[...]
    k2 = k * (1.0 + (a_lr - 1.0) * ka_ref[...])
    sums = segsum(jnp.concatenate([kk * kk, r * k2 * rk_ref[...]], axis=0))
    kk = kk / jnp.maximum(jnp.sqrt(sums[:bb * T]), 1e-12)
    bonus = sums[bb * T:] * v
    a_vec = -kk
    b_vec = kk * a_lr

    ri = lax.broadcasted_iota(jnp.int32, (T, T), 0)
    ci = lax.broadcasted_iota(jnp.int32, (T, T), 1)
    strict = ci < ri
    incl = ci <= ri
    eye = ci == ri
    tri_incl_bf = incl.astype(BF16)
    rc_xor = jnp.bitwise_xor(ri, ci)
    level_masks = []
    n = 1
    while n < T:
        level_masks.append(strict & (rc_xor >= n) & (rc_xor < 2 * n))
        n *= 2
    lane = lax.broadcasted_iota(jnp.int32, (T, LANES), 1)
    head0 = lane < HEAD_DIM
    pr = lax.broadcasted_iota(jnp.int32, (LANES, LANES), 0)
    pc = lax.broadcasted_iota(jnp.int32, (LANES, LANES), 1)
    same_head = (pr < HEAD_DIM) == (pc < HEAD_DIM)
    eye_p = pr == pc

    units = [(bi, p) for bi in range(bb) for p in range(n_pairs)]
    tile = lambda t, u: t[units[u][0] * T:(units[u][0] + 1) * T, units[u][1] * LANES:(units[u][1] + 1) * LANES]
    nu = range(len(units))
    heads = [(u, hh) for u in nu for hh in range(2)]
    zero = jnp.zeros((T, LANES), F32)

    lw_hi, lw_lo = _split2(lw)
    rt, vb, atb, dend_t, dt_row, sc = [], [], [], [], [], []
    for u in nu:
        lwu = tile(lw, u)
        clu = _dot(tri_incl_bf, tile(lw_hi, u)) + _dot(tri_incl_bf, tile(lw_lo, u))
        cl_last = clu[T - 1:T, :]
        dinv = jnp.exp(-clu)
        dend = jnp.exp(cl_last - clu)
        at = tile(a_vec, u) * jnp.exp(clu - lwu)
        bt = tile(b_vec, u) * dinv
        kt = tile(k2, u) * dinv
        rtu = tile(r, u) * jnp.exp(clu)
        rt.append(rtu)
        vb.append(tile(v, u).astype(BF16))
        atb.append(at.astype(BF16))
        dend_t.append(jnp.concatenate([(tile(b_vec, u) * dend).T, (tile(k2, u) * dend).T], axis=1).astype(BF16))
        dt_row.append(jnp.exp(cl_last))
        lhs = jnp.concatenate([jnp.where(head0, at, zero), jnp.where(head0, zero, at),
                               jnp.where(head0, rtu, zero), jnp.where(head0, zero, rtu)], axis=0).astype(BF16)
        rhs = jnp.concatenate([bt, kt], axis=0).astype(BF16)
        sc.append(_dot_nt(lhs, rhs))

    aab, aak, arbk = {}, {}, {}
    for (u, hh) in heads:
        aab[u, hh] = jnp.where(strict, sc[u][hh * T:(hh + 1) * T, 0:T], 0.0)
        aak[u, hh] = jnp.where(strict, sc[u][hh * T:(hh + 1) * T, T:2 * T], 0.0).astype(BF16)
        arbk[u, hh] = jnp.where(jnp.concatenate([incl, incl], axis=1), sc[u][(2 + hh) * T:(3 + hh) * T, :],
                                0.0).astype(BF16)

    tinv = {hd: jnp.where(eye, 1.0, jnp.where(level_masks[0], aab[hd], 0.0)).astype(BF16) for hd in heads}
    for lm in level_masks[1:]:
        half = {hd: _dot(tinv[hd], jnp.where(lm, aab[hd], 0.0).astype(BF16)).astype(BF16) for hd in heads}
        tinv = {hd: (tinv[hd].astype(F32) + _dot(half[hd], tinv[hd])).astype(BF16) for hd in heads}

    aakv = {hd: _dot(aak[hd], vb[hd[0]]).astype(BF16) for hd in heads}
    ap_h = {hd: _dot(tinv[hd], atb[hd[0]]) for hd in heads}
    w2_h = {hd: _dot(tinv[hd], aakv[hd]) for hd in heads}
    apn = [jnp.where(head0, ap_h[u, 0], ap_h[u, 1]).astype(BF16) for u in nu]
    w2v = [jnp.concatenate([jnp.where(head0, w2_h[u, 0], w2_h[u, 1]).astype(BF16), vb[u]], axis=0) for u in nu]
    qp_h = {hd: _dot(arbk[hd][:, 0:T], apn[hd[0]]) for hd in heads}
    yl_h = {hd: _dot(arbk[hd], w2v[hd[0]]) for hd in heads}

    ys = []
    for u in nu:
        qp = rt[u] + jnp.where(head0, qp_h[u, 0], qp_h[u, 1])
        yl = jnp.where(head0, yl_h[u, 0], yl_h[u, 1])
        g_mat = jnp.where(same_head, _dot(dend_t[u][:, 0:T], apn[u]), 0.0) + jnp.where(eye_p, dt_row[u], 0.0)
        c_mat = jnp.where(same_head, _dot(dend_t[u], w2v[u]), 0.0)
        m0 = m_ref[u].astype(BF16)
        ys.append(_dot(qp.astype(BF16), m0) + yl)
        m_ref[u] = _dot(g_mat.astype(BF16), m0) + c_mat

    y = jnp.concatenate([jnp.concatenate(ys[bi * n_pairs:(bi + 1) * n_pairs], axis=1) for bi in range(bb)],
                        axis=0)
    inv_n = 1.0 / HEAD_DIM
    mean = segsum(y) * inv_n
    yc = y - mean
    var = segsum(yc * yc) * inv_n
    yn = yc * lax.rsqrt(var + GN_EPS) * lng_ref[...] + lnb_ref[...]
    o_ref[...] = ((yn + bonus) * g).astype(BF16).reshape(bb, T, d_rw)


def _wkv(feat_rw, mu, w0, wlw, a0, wla, wlg, k_k, k_a, r_k, lnx_g, lnx_b, bd, d_rw, bb):
    B, Lp, c_rw = feat_rw.shape
    T = WKV_CHUNK
    vec = _const_spec((1, d_rw))
    return pl.pallas_call(
        functools.partial(_wkv_kernel, d_rw=d_rw, bb=bb),
        grid=(B // bb, Lp // T),
        in_specs=[
            pl.BlockSpec((bb, T, c_rw), lambda b, c: (b, c, 0)),
            _const_spec((1, c_rw)),
            vec, _const_spec((LANES, d_rw)), vec, _const_spec((LANES, d_rw)), _const_spec((LORA_G, d_rw)),
            vec, vec, vec, vec, vec,
            _const_spec((LANES, LANES)),
        ],
        out_specs=pl.BlockSpec((bb, T, d_rw), lambda b, c: (b, c, 0)),
        out_shape=jax.ShapeDtypeStruct((B, Lp, d_rw), BF16),
        scratch_shapes=[pltpu.VMEM((bb * (d_rw // LANES), LANES, LANES), F32),
                        pltpu.VMEM((bb, 1, c_rw), F32)],
        compiler_params=_params(("parallel", "arbitrary")),
        name="wkv",
    )(feat_rw, mu, w0, wlw, a0, wla, wlg, k_k, k_a, r_k, lnx_g, lnx_b, bd)


def _fxprep_kernel(h_ref, g_ref, wfx_ref, qg_ref, kg_ref, bf_ref, sq_ref, sk_ref, cq_ref, ck_ref,
                   q_ref, kt_ref, v_ref, f2_ref, carry_ref, *, n_heads):
    i = pl.program_id(1)

    @pl.when(i == 0)
    def _():
        carry_ref[...] = jnp.zeros_like(carry_ref)

    d_fx = n_heads * HEAD_DIM
    x = _dot(_rmsnorm_bf16(h_ref[...], g_ref[...]), wfx_ref[...])
    tm = x.shape[0]
    fl = x[:, 3 * d_fx:3 * d_fx + LANES]
    z = fl + bf_ref[...]
    log_f = jnp.minimum(z, 0.0) - jnp.log(1.0 + jnp.exp(-jnp.abs(z)))
    lane = lax.broadcasted_iota(jnp.int32, (tm, LANES), 1)
    log_f = jnp.where(lane < n_heads, log_f, 0.0)
    ri = lax.broadcasted_iota(jnp.int32, (tm, tm), 0)
    ci = lax.broadcasted_iota(jnp.int32, (tm, tm), 1)
    tri = (ci <= ri).astype(BF16)
    fcum = carry_ref[...] + _dot_exact_lhs(tri, log_f)
    carry_ref[...] = fcum[tm - 1:tm, :]
    f2 = fcum * LOG2E
    f2_ref[...] = f2
    hi, mid, lo = _split3(f2)
    parts = (hi.astype(F32) + pltpu.roll(mid.astype(F32), n_heads, 1)
             + pltpu.roll(lo.astype(F32), 2 * n_heads, 1)).astype(BF16)
    extra_q = _dot(parts, sq_ref[...]) + cq_ref[...]
    extra_k = _dot(parts, sk_ref[...]) + ck_ref[...]

    low = lane < HEAD_DIM
    inv_n = 1.0 / HEAD_DIM

    def pair_norm(p, gain):
        p2 = p * p
        ms_lo = jnp.sum(jnp.where(low, p2, 0.0), axis=-1, keepdims=True) * inv_n
        ms_hi = jnp.sum(jnp.where(low, 0.0, p2), axis=-1, keepdims=True) * inv_n
        return p * jnp.where(low, lax.rsqrt(ms_lo + NORM_EPS), lax.rsqrt(ms_hi + NORM_EPS)) * gain

    for kp in range(n_heads // 2):
        ps = slice(kp * LANES, (kp + 1) * LANES)
        te = slice(2 * kp * LANES, (2 * kp + 1) * LANES)
        to = slice((2 * kp + 1) * LANES, (2 * kp + 2) * LANES)
        qn = pair_norm(x[:, ps], qg_ref[:, ps])
        kn = pair_norm(x[:, d_fx + kp * LANES:d_fx + (kp + 1) * LANES], kg_ref[:, ps])
        vp = x[:, 2 * d_fx + kp * LANES:2 * d_fx + (kp + 1) * LANES]
        q_ref[:, te] = jnp.where(low, qn, extra_q[:, te]).astype(BF16)
        q_ref[:, to] = jnp.where(low, extra_q[:, to], qn).astype(BF16)
        kt_ref[te, :] = jnp.where(low, kn, extra_k[:, te]).T.astype(BF16)
        kt_ref[to, :] = jnp.where(low, extra_k[:, to], kn).T.astype(BF16)
        v_ref[:, te] = jnp.where(low, vp, 1.0).astype(BF16)
        v_ref[:, to] = jnp.where(low, 1.0, vp).astype(BF16)


def _fxprep(h3, g, wfx, qg, kg, bf, sq, sk, cq, ck, n_heads, tm):
    B, Lp, d = h3.shape
    n_fx = wfx.shape[1]
    d_fx = n_heads * HEAD_DIM
    hw = n_heads * LANES
    return pl.pallas_call(
        functools.partial(_fxprep_kernel, n_heads=n_heads),
        grid=(B, Lp // tm),
        in_specs=[
            pl.BlockSpec((None, tm, d), lambda b, i: (b, i, 0)),
            _const_spec((1, d)), _const_spec((d, n_fx)),
            _const_spec((1, d_fx)), _const_spec((1, d_fx)), _const_spec((1, LANES)),
            _const_spec((LANES, hw)), _const_spec((LANES, hw)),
            _const_spec((1, hw)), _const_spec((1, hw)),
        ],
        out_specs=[
            pl.BlockSpec((None, tm, hw), lambda b, i: (b, i, 0)),
            pl.BlockSpec((None, hw, tm), lambda b, i: (b, 0, i)),
            pl.BlockSpec((None, tm, hw), lambda b, i: (b, i, 0)),
            pl.BlockSpec((None, tm, LANES), lambda b, i: (b, i, 0)),
        ],
        out_shape=[
            jax.ShapeDtypeStruct((B, Lp, hw), BF16),
            jax.ShapeDtypeStruct((B, hw, Lp), BF16),
            jax.ShapeDtypeStruct((B, Lp, hw), BF16),
            jax.ShapeDtypeStruct((B, Lp, LANES), F32),
        ],
        scratch_shapes=[pltpu.VMEM((1, LANES), F32)],
        compiler_params=_params(("parallel", "arbitrary")),
        name="fxprep",
    )(h3, g, wfx, qg, kg, bf, sq, sk, cq, ck)


def _attn_kernel(jlo_ref, q_ref, kt_ref, v_ref, o_ref, acc_scr, m_scr, *, blk, nh, online):
    b, g, i = pl.program_id(0), pl.program_id(1), pl.program_id(2)
    lo = jlo_ref[(b * pl.num_programs(1) + g) * pl.num_programs(2) + i]
    ri = lax.broadcasted_iota(jnp.int32, (blk, blk), 0)
    ci = lax.broadcasted_iota(jnp.int32, (blk, blk), 1)
    causal = ci <= ri
    lane = lax.broadcasted_iota(jnp.int32, (blk, LANES), 1)
    hs = [slice(h * LANES, (h + 1) * LANES) for h in range(nh)]

    def block(j, diagonal):
        off = pl.multiple_of(j * blk, blk)
        s = [_dot(q_ref[:, hs[h]], kt_ref[hs[h], pl.ds(off, blk)]) for h in range(nh)]
        probs, alphas = [], []
        for h in range(nh):
            sm = jnp.where(causal, s[h], NEG_BIG) if diagonal else s[h]
            if online:
                row_max = jnp.max(sm, axis=-1, keepdims=True)
                if diagonal:
                    m_new = row_max
                else:
                    m_old = m_scr[h]
                    m_new = jnp.maximum(m_old, row_max)
                    alphas.append(jnp.exp2(m_old - m_new))
                m_scr[h] = m_new
                sm = sm - m_new
            probs.append(jnp.exp2(sm).astype(BF16))
        for h in range(nh):
            pv = _dot(probs[h], v_ref[pl.ds(off, blk), hs[h]])
            if diagonal:
                acc_scr[h] = pv
            elif online:
                acc_scr[h] = alphas[h] * acc_scr[h] + pv
            else:
                acc_scr[h] += pv

    block(i, True)

    def body(n, carry):
        block(i - 1 - n, False)
        return carry

    lax.fori_loop(0, i - lo, body, 0)
    pairs = []
    for k in range(nh // 2):
        even, odd = acc_scr[2 * k], acc_scr[2 * k + 1]
        num = jnp.where(lane < HEAD_DIM, even, odd)
        den = pltpu.roll(jnp.where(lane < HEAD_DIM, odd, even), HEAD_DIM, 1)
        pairs.append(num / den)
    o_ref[...] = jnp.concatenate(pairs, axis=1).astype(BF16)


def _attn(jlo, qp, ktp, vp, blk, nh, online):
    B, Lp, hw = qp.shape
    n_groups = hw // (nh * LANES)
    gw = nh * LANES
    grid_spec = pltpu.PrefetchScalarGridSpec(
        num_scalar_prefetch=1,
        grid=(B, n_groups, Lp // blk),
        in_specs=[
            pl.BlockSpec((None, blk, gw), lambda b, g, i, jlo_ref: (b, i, g)),
            pl.BlockSpec((None, gw, Lp), lambda b, g, i, jlo_ref: (b, g, 0)),
            pl.BlockSpec((None, Lp, gw), lambda b, g, i, jlo_ref: (b, 0, g)),
        ],
        out_specs=pl.BlockSpec((None, blk, gw // 2), lambda b, g, i, jlo_ref: (b, i, g)),
        scratch_shapes=[pltpu.VMEM((nh, blk, LANES), F32), pltpu.VMEM((nh, blk, 1), F32)],
    )
    return pl.pallas_call(
        functools.partial(_attn_kernel, blk=blk, nh=nh, online=online),
        grid_spec=grid_spec,
        out_shape=jax.ShapeDtypeStruct((B, Lp, hw // 2), BF16),
        compiler_params=_params(("parallel", "parallel", "arbitrary")),
        name="attn_online" if online else "attn",
    )(jlo, qp, ktp, vp)


def _first_live_block(f2, cmax, blk, n_heads, group):
    B, Lp, _ = f2.shape
    nq = Lp // blk
    f_first = f2[:, 0::blk, :n_heads]
    f_last = f2[:, blk - 1::blk, :n_heads]
    gap = f_first[:, :, None, :] - f_last[:, None, :, :]
    jj = jnp.arange(nq)
    live = (gap + 2.0 * cmax >= -SKIP_LOG2) | (jj[None, None, :, None] >= jj[None, :, None, None])
    jlo = jnp.argmax(live, axis=2).astype(jnp.int32)
    jlo = jnp.min(jlo.reshape(B, nq, n_heads // group, group), axis=3)
    return jnp.transpose(jlo, (0, 2, 1)).reshape(-1)


def _outmlp_kernel(h_ref, yrw_ref, yfx_ref, gate_ref, wrw_ref, wfx_ref, wo_ref, gm_ref, wup_ref,
                   wdn_ref, o_ref, *, ff_chunk):
    d = h_ref.shape[1]
    gates = gate_ref[...].astype(F32)
    merged = gates[:, :d] * _dot(yrw_ref[...], wrw_ref[...]) + gates[:, d:] * _dot(yfx_ref[...], wfx_ref[...])
    h1 = h_ref[...] + _dot(merged.astype(BF16), wo_ref[...])
    ms = jnp.mean(h1 * h1, axis=-1, keepdims=True)
    z = (h1 * lax.rsqrt(ms + NORM_EPS) * gm_ref[...]).astype(BF16)
    acc = h1
    for c0 in range(0, wup_ref.shape[1], ff_chunk):
        u = jnp.maximum(_dot(z, wup_ref[:, c0:c0 + ff_chunk]), 0.0)
        acc = acc + _dot((u * u).astype(BF16), wdn_ref[c0:c0 + ff_chunk, :])
    o_ref[...] = acc


def _outmlp(h2, yrw, yfx, gates, wrw, wfx, wo, gm, wup, wdn, tm):
    rows, d = h2.shape
    d_rw, d_fx, d_ff = wrw.shape[0], wfx.shape[0], wup.shape[1]
    return pl.pallas_call(
        functools.partial(_outmlp_kernel, ff_chunk=min(d_ff, 1024)),
        grid=(rows // tm,),
        in_specs=[
            pl.BlockSpec((tm, d), lambda i: (i, 0)),
            pl.BlockSpec((tm, d_rw), lambda i: (i, 0)),
            pl.BlockSpec((tm, d_fx), lambda i: (i, 0)),
            pl.BlockSpec((tm, 2 * d), lambda i: (i, 0)),
            _const_spec((d_rw, d)), _const_spec((d_fx, d)), _const_spec((d, d)),
            _const_spec((1, d)), _const_spec((d, d_ff)), _const_spec((d_ff, d)),
        ],
        out_specs=pl.BlockSpec((tm, d), lambda i: (i, 0)),
        out_shape=jax.ShapeDtypeStruct((rows, d), F32),
        compiler_params=_params(("parallel",)),
        name="outmlp",
    )(h2, yrw, yfx, gates, wrw, wfx, wo, gm, wup, wdn)


def _largest_divisor(n, candidates):
    for c in candidates:
        if n % c == 0:
            return c
    raise ValueError(f"no tile in {candidates} divides {n}")


def _attn_constants(n_heads, shift):
    hw = n_heads * LANES
    sq = np.zeros((LANES, hw), np.float32)
    sk = np.zeros((LANES, hw), np.float32)
    cq = np.zeros((1, hw), np.float32)
    cq_shift = np.zeros((1, hw), np.float32)
    ck = np.zeros((1, hw), np.float32)
    for h in range(n_heads):
        base = h * LANES + (HEAD_DIM if h % 2 == 0 else 0)
        for part in range(3):
            sq[part * n_heads + h, base + part] = 1.0
            sk[part * n_heads + h, base + 3 + part] = -1.0
            cq[0, base + 3 + part] = 1.0
            ck[0, base + part] = 1.0
        cq_shift[0, base + 6] = -1.0
        ck[0, base + 6] = 1.0
    return (jnp.asarray(sq, BF16), jnp.asarray(sk, BF16), jnp.asarray(cq) + shift * jnp.asarray(cq_shift),
            jnp.asarray(ck))


def kernel(x, meta, norm_mix, w_in, b_gate, b_f, tm_mu, w0, w_lora_up, a0, a_lora_up, g_lora_up, k_k, k_a,
           r_k, lnx_g, lnx_b, q_gain, k_gain, w_out_rw, w_out_fx, w_o, norm_mlp, w_up, w_down):
    B, S, D = x.shape
    depth = w_in.shape[0]
    d_rw = w0.shape[1]
    d_fx = w_out_fx.shape[1]
    h_fx = b_f.shape[1]
    c_rw = 3 * d_rw + LORA_W + LORA_A + LORA_G
    c_fx = 3 * d_fx + h_fx
    assert d_rw % LANES == 0 and h_fx % 2 == 0 and d_fx == h_fx * HEAD_DIM and 3 * h_fx <= LANES
    assert LORA_W + LORA_A == LANES and LORA_G == LANES and HEAD_DIM * 2 == LANES

    L = N_META + S
    Lp = -(-L // SEQ_ALIGN) * SEQ_ALIGN
    rows = B * Lp
    tm_row = _largest_divisor(rows, (512, 256, 128))
    tm_prep = _largest_divisor(Lp, (256, 128))
    blk = SEQ_ALIGN
    wkv_bb = _largest_divisor(B, (WKV_BATCH_ROWS, 1))

    meta_b = jnp.broadcast_to(meta.astype(x.dtype)[None], (B, N_META, D))
    h = jnp.concatenate([meta_b, x, jnp.zeros((B, Lp - L, D), x.dtype)], axis=1).reshape(rows, D)

    seg = jnp.arange(LANES) // HEAD_DIM
    bd = (seg[:, None] == seg[None, :]).astype(BF16)
    zeros_lora = jnp.zeros((LORA_W, d_rw), F32)
    scale = HEAD_DIM ** -0.5
    row = lambda t: t.reshape(1, -1)

    for l in range(depth):
        wl = w_in[l]
        w_rw = wl[:, :c_rw].astype(BF16)
        w_fx = jnp.pad(wl[:, c_rw:c_rw + c_fx], ((0, 0), (0, LANES - h_fx))).astype(BF16)
        w_g = wl[:, c_rw + c_fx:].astype(BF16)

        feat_rw, gates = _inproj(h, row(norm_mix[l]), w_rw, w_g, row(b_gate[l]), tm_row)

        y_rw = _wkv(
            feat_rw.reshape(B, Lp, c_rw), row(tm_mu[l]), row(w0[l]),
            jnp.concatenate([w_lora_up[l], zeros_lora], axis=0).astype(BF16), row(a0[l]),
            jnp.concatenate([zeros_lora, a_lora_up[l]], axis=0).astype(BF16), g_lora_up[l].astype(BF16),
            row(k_k[l]), row(k_a[l]), row(r_k[l]), row(lnx_g[l]), row(lnx_b[l]), bd, d_rw, wkv_bb)

        qg = jnp.tile(q_gain[l] * (scale * LOG2E), h_fx)[None, :]
        kg = jnp.tile(k_gain[l], h_fx)[None, :]
        bf = jnp.pad(b_f[l], (0, LANES - h_fx))[None, :]
        cmax = 1.02 * HEAD_DIM * jnp.max(jnp.abs(qg)) * jnp.max(jnp.abs(kg))
        shift = (1.01 * cmax).astype(BF16).astype(F32)
        sq, sk, cq, ck = _attn_constants(h_fx, shift)
        qp, ktp, vp, f2 = _fxprep(h.reshape(B, Lp, D), row(norm_mix[l]), w_fx, qg, kg, bf, sq, sk, cq, ck,
                                  h_fx, tm_prep)
        jlo = _first_live_block(f2, cmax, blk, h_fx, ATTN_GROUP)
        y_fx = lax.cond(
            cmax <= FIXED_SHIFT_MAX_LOG2,
            lambda ops: _attn(*ops, blk, ATTN_GROUP, False),
            lambda ops: _attn(*ops, blk, ATTN_GROUP, True),
            (jlo, qp, ktp, vp))

        h = _outmlp(h, y_rw.reshape(rows, d_rw), y_fx.reshape(rows, d_fx), gates,
                    w_out_rw[l].astype(BF16), w_out_fx[l].astype(BF16), w_o[l].astype(BF16),
                    row(norm_mlp[l]), w_up[l].astype(BF16), w_down[l].astype(BF16), tm_row)

    return h.reshape(B, Lp, D)[:, N_META:N_META + S]
```

```python
import functools

import jax
import jax.numpy as jnp
import numpy as np
from jax import lax
from jax.experimental import pallas as pl
from jax.experimental.pallas import tpu as pltpu

F32 = jnp.float32
BF16 = jnp.bfloat16

N_META = 16
HEAD_DIM = 64
LORA_W = 64
LORA_A = 64
LORA_G = 128
NORM_EPS = 1e-6
GN_EPS = 64e-5
DECAY_OFFSET = 0.5
LOG2E = 1.4426950408889634

LANES = 128
SEQ_ALIGN = 256
SKIP_LOG2 = 160.0
WKV_CHUNK = 128
WKV_BATCH_ROWS = 2
ATTN_GROUP = 4
FIXED_SHIFT_MAX_LOG2 = 50.0
VMEM_LIMIT_BYTES = 56 * 1024 * 1024
NEG_BIG = -1e30


def _dot(a, b):
    return jnp.dot(a, b, preferred_element_type=F32)


def _dot_nt(a, b):
    return lax.dot_general(a, b, (((1,), (1,)), ((), ())), preferred_element_type=F32)


def _split2(x):
    hi = x.astype(BF16)
    lo = (x - hi.astype(F32)).astype(BF16)
    return hi, lo


def _split3(x):
    hi = x.astype(BF16)
    r1 = x - hi.astype(F32)
    mid = r1.astype(BF16)
    lo = (r1 - mid.astype(F32)).astype(BF16)
    return hi, mid, lo


def _dot_exact_lhs(m_bf16, x):
    hi, mid, lo = _split3(x)
    return _dot(m_bf16, hi) + _dot(m_bf16, mid) + _dot(m_bf16, lo)


def _sigmoid(x):
    return 1.0 / (1.0 + jnp.exp(-x))


def _softplus(x):
    return jnp.maximum(x, 0.0) + jnp.log(1.0 + jnp.exp(-jnp.abs(x)))


def _const_spec(shape):
    nd = len(shape)
    return pl.BlockSpec(shape, lambda *_: (0,) * nd, pipeline_mode=pl.Buffered(1))


def _params(sem):
    return pltpu.CompilerParams(dimension_semantics=sem, vmem_limit_bytes=VMEM_LIMIT_BYTES)


def _rmsnorm_bf16(x, g):
    ms = jnp.mean(x * x, axis=-1, keepdims=True)
    return (x * lax.rsqrt(ms + NORM_EPS) * g).astype(BF16)


def _inproj_kernel(h_ref, g_ref, wrw_ref, wg_ref, bg_ref, u_ref, rw_ref, gate_ref):
    u = _rmsnorm_bf16(h_ref[...], g_ref[...])
    u_ref[...] = u
    rw_ref[...] = _dot(u, wrw_ref[...])
    gate_ref[...] = _sigmoid(_dot(u, wg_ref[...]) + bg_ref[...]).astype(BF16)


def _inproj(h2, g, wrw, wg, bg, tm):
    rows, d = h2.shape
    n_rw, n_g = wrw.shape[1], wg.shape[1]
    return pl.pallas_call(
        _inproj_kernel,
        grid=(rows // tm,),
        in_specs=[
            pl.BlockSpec((tm, d), lambda i: (i, 0)),
            _const_spec((1, d)),
            _const_spec((d, n_rw)),
            _const_spec((d, n_g)),
            _const_spec((1, n_g)),
        ],
        out_specs=[
            pl.BlockSpec((tm, d), lambda i: (i, 0)),
            pl.BlockSpec((tm, n_rw), lambda i: (i, 0)),
            pl.BlockSpec((tm, n_g), lambda i: (i, 0)),
        ],
        out_shape=[
            jax.ShapeDtypeStruct((rows, d), BF16),
            jax.ShapeDtypeStruct((rows, n_rw), F32),
            jax.ShapeDtypeStruct((rows, n_g), BF16),
        ],
        compiler_params=_params(("parallel",)),
        name="inproj",
    )(h2, g, wrw, wg, bg)


def _wkv_kernel(x_ref, mu_ref, w0_ref, wlw_ref, a0_ref, wla_ref, wlg_ref, kk_ref, ka_ref, rk_ref,
                lng_ref, lnb_ref, bd_ref, tri_ref, o_ref, m_ref, prev_ref, *, d_rw, bb):
    T = WKV_CHUNK
    c = pl.program_id(1)
    n_pairs = d_rw // LANES
    c_rw = x_ref.shape[2]

    @pl.when(c == 0)
    def _():
        m_ref[...] = jnp.zeros_like(m_ref)
        prev_ref[...] = jnp.zeros_like(prev_ref)

    x = x_ref[...].reshape(bb * T, c_rw)
    row = lax.broadcasted_iota(jnp.int32, x.shape, 0)
    shifted = pltpu.roll(x, 1, 0)
    for bi in range(bb):
        shifted = jnp.where(row == bi * T, prev_ref[bi], shifted)
        prev_ref[bi] = x[(bi + 1) * T - 1:(bi + 1) * T, :]
    f = x + mu_ref[...] * (shifted - x)

    r = f[:, 0:d_rw]
    k = f[:, d_rw:2 * d_rw]
    v = f[:, 2 * d_rw:3 * d_rw]
    wa_d = f[:, 3 * d_rw:3 * d_rw + LORA_W + LORA_A]
    g_d = f[:, 3 * d_rw + LORA_W + LORA_A:]

    bd = bd_ref[...]

    def segsum(t):
        stacked = jnp.concatenate([t[:, p * LANES:(p + 1) * LANES] for p in range(n_pairs)], axis=0)
        out = _dot(stacked.astype(BF16), bd)
        rows = t.shape[0]
        return jnp.concatenate([out[p * rows:(p + 1) * rows] for p in range(n_pairs)], axis=1)

    w_log = -_softplus(-(w0_ref[...] + _dot(jnp.tanh(wa_d).astype(BF16), wlw_ref[...]))) - DECAY_OFFSET
    lw = -jnp.exp(w_log)
    a_lr = _sigmoid(a0_ref[...] + _dot(wa_d.astype(BF16), wla_ref[...]))
    g = _dot(_sigmoid(g_d).astype(BF16), wlg_ref[...])
    kk = k * kk_ref[...]
    k2 = k * (1.0 + (a_lr - 1.0) * ka_ref[...])
    sums = segsum(jnp.concatenate([kk * kk, r * k2 * rk_ref[...]], axis=0))
    kk = kk * lax.rsqrt(jnp.maximum(sums[:bb * T], 1e-24))
    bonus = sums[bb * T:] * v
    a_vec = -kk
    b_vec = kk * a_lr

    ri = lax.broadcasted_iota(jnp.int32, (T, T), 0)
    ci = lax.broadcasted_iota(jnp.int32, (T, T), 1)
    strict = ci < ri
    incl = ci <= ri
    eye = ci == ri
    tri_incl_bf = tri_ref[...]
    rc_xor = jnp.bitwise_xor(ri, ci)
    level_masks = []
    n = 1
    while n < T:
        level_masks.append(strict & (rc_xor >= n) & (rc_xor < 2 * n))
        n *= 2
    lane = lax.broadcasted_iota(jnp.int32, (T, LANES), 1)
    head0 = lane < HEAD_DIM
    pr = lax.broadcasted_iota(jnp.int32, (LANES, LANES), 0)
    pc = lax.broadcasted_iota(jnp.int32, (LANES, LANES), 1)
    same_head = (pr < HEAD_DIM) == (pc < HEAD_DIM)
    eye_p = pr == pc

    units = [(bi, p) for bi in range(bb) for p in range(n_pairs)]
    tile = lambda t, u: t[units[u][0] * T:(units[u][0] + 1) * T, units[u][1] * LANES:(units[u][1] + 1) * LANES]
    nu = range(len(units))
    heads = [(u, hh) for u in nu for hh in range(2)]
    zero = jnp.zeros((T, LANES), F32)

    lw_hi, lw_lo = _split2(lw)
    rt, vb, atb, dend_t, dt_row, sc = [], [], [], [], [], []
    for u in nu:
        lwu = tile(lw, u)
        clu = _dot(tri_incl_bf, tile(lw_hi, u)) + _dot(tri_incl_bf, tile(lw_lo, u))
        cl_last = clu[T - 1:T, :]
        dinv = jnp.exp(-clu)
        dend = jnp.exp(cl_last - clu)
        at = tile(a_vec, u) * jnp.exp(clu - lwu)
        bt = tile(b_vec, u) * dinv
        kt = tile(k2, u) * dinv
        rtu = tile(r, u) * jnp.exp(clu)
        rt.append(rtu)
        vb.append(tile(v, u).astype(BF16))
        atb.append(at.astype(BF16))
        dend_t.append(jnp.concatenate([(tile(b_vec, u) * dend).T, (tile(k2, u) * dend).T], axis=1).astype(BF16))
        dt_row.append(jnp.exp(cl_last))
        lhs = jnp.concatenate([jnp.where(head0, at, zero), jnp.where(head0, zero, at),
                               jnp.where(head0, rtu, zero), jnp.where(head0, zero, rtu)], axis=0).astype(BF16)
        rhs = jnp.concatenate([bt, kt], axis=0).astype(BF16)
        sc.append(_dot_nt(lhs, rhs))

    aab, aak, arbk = {}, {}, {}
    for (u, hh) in heads:
        aab[u, hh] = jnp.where(strict, sc[u][hh * T:(hh + 1) * T, 0:T], 0.0)
        aak[u, hh] = jnp.where(strict, sc[u][hh * T:(hh + 1) * T, T:2 * T], 0.0).astype(BF16)
        arbk[u, hh] = jnp.where(jnp.concatenate([incl, incl], axis=1), sc[u][(2 + hh) * T:(3 + hh) * T, :],
                                0.0).astype(BF16)

    tinv = {hd: jnp.where(eye, 1.0, jnp.where(level_masks[0], aab[hd], 0.0)).astype(BF16) for hd in heads}
    for lm in level_masks[1:]:
        half = {hd: _dot(tinv[hd], jnp.where(lm, aab[hd], 0.0).astype(BF16)).astype(BF16) for hd in heads}
        tinv = {hd: (tinv[hd].astype(F32) + _dot(half[hd], tinv[hd])).astype(BF16) for hd in heads}

    aakv = {hd: _dot(aak[hd], vb[hd[0]]).astype(BF16) for hd in heads}
    ap_h = {hd: _dot(tinv[hd], atb[hd[0]]) for hd in heads}
    w2_h = {hd: _dot(tinv[hd], aakv[hd]) for hd in heads}
    apn = [jnp.where(head0, ap_h[u, 0], ap_h[u, 1]).astype(BF16) for u in nu]
    w2v = [jnp.concatenate([jnp.where(head0, w2_h[u, 0], w2_h[u, 1]).astype(BF16), vb[u]], axis=0) for u in nu]
    qp_h = {hd: _dot(arbk[hd][:, 0:T], apn[hd[0]]) for hd in heads}
    yl_h = {hd: _dot(arbk[hd], w2v[hd[0]]) for hd in heads}

    ys = []
    for u in nu:
        qp = rt[u] + jnp.where(head0, qp_h[u, 0], qp_h[u, 1])
        yl = jnp.where(head0, yl_h[u, 0], yl_h[u, 1])
        g_mat = jnp.where(same_head, _dot(dend_t[u][:, 0:T], apn[u]), 0.0) + jnp.where(eye_p, dt_row[u], 0.0)
        c_mat = jnp.where(same_head, _dot(dend_t[u], w2v[u]), 0.0)
        m0 = m_ref[u].astype(BF16)
        ys.append(_dot(qp.astype(BF16), m0) + yl)
        m_ref[u] = _dot(g_mat.astype(BF16), m0) + c_mat

    y = jnp.concatenate([jnp.concatenate(ys[bi * n_pairs:(bi + 1) * n_pairs], axis=1) for bi in range(bb)],
                        axis=0)
    inv_n = 1.0 / HEAD_DIM
    mean = segsum(y) * inv_n
    yc = y - mean
    var = segsum(yc * yc) * inv_n
    yn = yc * lax.rsqrt(var + GN_EPS) * lng_ref[...] + lnb_ref[...]
    o_ref[...] = ((yn + bonus) * g).astype(BF16).reshape(bb, T, d_rw)


def _wkv(feat_rw, mu, w0, wlw, a0, wla, wlg, k_k, k_a, r_k, lnx_g, lnx_b, bd, tri, d_rw, bb):
    B, Lp, c_rw = feat_rw.shape
    T = WKV_CHUNK
    vec = _const_spec((1, d_rw))
    return pl.pallas_call(
        functools.partial(_wkv_kernel, d_rw=d_rw, bb=bb),
        grid=(B // bb, Lp // T),
        in_specs=[
            pl.BlockSpec((bb, T, c_rw), lambda b, c: (b, c, 0)),
            _const_spec((1, c_rw)),
            vec, _const_spec((LANES, d_rw)), vec, _const_spec((LANES, d_rw)), _const_spec((LORA_G, d_rw)),
            vec, vec, vec, vec, vec,
            _const_spec((LANES, LANES)), _const_spec((T, T)),
        ],
        out_specs=pl.BlockSpec((bb, T, d_rw), lambda b, c: (b, c, 0)),
        out_shape=jax.ShapeDtypeStruct((B, Lp, d_rw), BF16),
        scratch_shapes=[pltpu.VMEM((bb * (d_rw // LANES), LANES, LANES), F32),
                        pltpu.VMEM((bb, 1, c_rw), F32)],
        compiler_params=_params(("parallel", "arbitrary")),
        name="wkv",
    )(feat_rw, mu, w0, wlw, a0, wla, wlg, k_k, k_a, r_k, lnx_g, lnx_b, bd, tri)


def _fxprep_kernel(u_ref, wfx_ref, qg_ref, kg_ref, bf_ref, sq_ref, sk_ref, cq_ref, ck_ref, tri_ref,
                   q_ref, kt_ref, v_ref, f2_ref, carry_ref, *, n_heads):
    i = pl.program_id(1)

    @pl.when(i == 0)
    def _():
        carry_ref[...] = jnp.zeros_like(carry_ref)

    d_fx = n_heads * HEAD_DIM
    x = _dot(u_ref[...], wfx_ref[...])
    tm = x.shape[0]
    fl = x[:, 3 * d_fx:3 * d_fx + LANES]
    z = fl + bf_ref[...]
    log_f = jnp.minimum(z, 0.0) - jnp.log(1.0 + jnp.exp(-jnp.abs(z)))
    lane = lax.broadcasted_iota(jnp.int32, (tm, LANES), 1)
    log_f = jnp.where(lane < n_heads, log_f, 0.0)
    fcum = carry_ref[...] + _dot_exact_lhs(tri_ref[...], log_f)
    carry_ref[...] = fcum[tm - 1:tm, :]
    f2 = fcum * LOG2E
    f2_ref[...] = f2
    hi, mid, lo = _split3(f2)
    parts = (hi.astype(F32) + pltpu.roll(mid.astype(F32), n_heads, 1)
             + pltpu.roll(lo.astype(F32), 2 * n_heads, 1)).astype(BF16)
    extra_q = _dot(parts, sq_ref[...]) + cq_ref[...]
    extra_k = _dot(parts, sk_ref[...]) + ck_ref[...]

    low = lane < HEAD_DIM
    inv_n = 1.0 / HEAD_DIM

    def pair_norm(p, gain):
        p2 = p * p
        ms_lo = jnp.sum(jnp.where(low, p2, 0.0), axis=-1, keepdims=True) * inv_n
        ms_hi = jnp.sum(jnp.where(low, 0.0, p2), axis=-1, keepdims=True) * inv_n
        return p * jnp.where(low, lax.rsqrt(ms_lo + NORM_EPS), lax.rsqrt(ms_hi + NORM_EPS)) * gain

    for kp in range(n_heads // 2):
        ps = slice(kp * LANES, (kp + 1) * LANES)
        te = slice(2 * kp * LANES, (2 * kp + 1) * LANES)
        to = slice((2 * kp + 1) * LANES, (2 * kp + 2) * LANES)
        qn = pair_norm(x[:, ps], qg_ref[:, ps])
        kn = pair_norm(x[:, d_fx + kp * LANES:d_fx + (kp + 1) * LANES], kg_ref[:, ps])
        vp = x[:, 2 * d_fx + kp * LANES:2 * d_fx + (kp + 1) * LANES]
        q_ref[:, te] = jnp.where(low, qn, extra_q[:, te]).astype(BF16)
        q_ref[:, to] = jnp.where(low, extra_q[:, to], qn).astype(BF16)
        kt_ref[te, :] = jnp.where(low, kn, extra_k[:, te]).T.astype(BF16)
        kt_ref[to, :] = jnp.where(low, extra_k[:, to], kn).T.astype(BF16)
        v_ref[:, te] = jnp.where(low, vp, 1.0).astype(BF16)
        v_ref[:, to] = jnp.where(low, 1.0, vp).astype(BF16)


def _fxprep(u3, wfx, qg, kg, bf, sq, sk, cq, ck, tri, n_heads, tm):
    B, Lp, d = u3.shape
    n_fx = wfx.shape[1]
    d_fx = n_heads * HEAD_DIM
    hw = n_heads * LANES
    return pl.pallas_call(
        functools.partial(_fxprep_kernel, n_heads=n_heads),
        grid=(B, Lp // tm),
        in_specs=[
            pl.BlockSpec((None, tm, d), lambda b, i: (b, i, 0)),
            _const_spec((d, n_fx)),
            _const_spec((1, d_fx)), _const_spec((1, d_fx)), _const_spec((1, LANES)),
            _const_spec((LANES, hw)), _const_spec((LANES, hw)),
            _const_spec((1, hw)), _const_spec((1, hw)), _const_spec((tm, tm)),
        ],
        out_specs=[
            pl.BlockSpec((None, tm, hw), lambda b, i: (b, i, 0)),
            pl.BlockSpec((None, hw, tm), lambda b, i: (b, 0, i)),
            pl.BlockSpec((None, tm, hw), lambda b, i: (b, i, 0)),
            pl.BlockSpec((None, tm, LANES), lambda b, i: (b, i, 0)),
        ],
        out_shape=[
            jax.ShapeDtypeStruct((B, Lp, hw), BF16),
            jax.ShapeDtypeStruct((B, hw, Lp), BF16),
            jax.ShapeDtypeStruct((B, Lp, hw), BF16),
            jax.ShapeDtypeStruct((B, Lp, LANES), F32),
        ],
        scratch_shapes=[pltpu.VMEM((1, LANES), F32)],
        compiler_params=_params(("parallel", "arbitrary")),
        name="fxprep",
    )(u3, wfx, qg, kg, bf, sq, sk, cq, ck, tri)


def _attn_kernel(jlo_ref, q_ref, kt_ref, v_ref, o_ref, acc_scr, m_scr, *, blk, nh, online):
    b, g, i = pl.program_id(0), pl.program_id(1), pl.program_id(2)
    lo = jlo_ref[(b * pl.num_programs(1) + g) * pl.num_programs(2) + i]
    ri = lax.broadcasted_iota(jnp.int32, (blk, blk), 0)
    ci = lax.broadcasted_iota(jnp.int32, (blk, blk), 1)
    causal = ci <= ri
    lane = lax.broadcasted_iota(jnp.int32, (blk, LANES), 1)
    hs = [slice(h * LANES, (h + 1) * LANES) for h in range(nh)]

    def block(j, diagonal):
        off = pl.multiple_of(j * blk, blk)
        s = [_dot(q_ref[:, hs[h]], kt_ref[hs[h], pl.ds(off, blk)]) for h in range(nh)]
        probs, alphas = [], []
        for h in range(nh):
            sm = jnp.where(causal, s[h], NEG_BIG) if diagonal else s[h]
            if online:
                row_max = jnp.max(sm, axis=-1, keepdims=True)
                if diagonal:
                    m_new = row_max
                else:
                    m_old = m_scr[h]
                    m_new = jnp.maximum(m_old, row_max)
                    alphas.append(jnp.exp2(m_old - m_new))
                m_scr[h] = m_new
                sm = sm - m_new
            probs.append(jnp.exp2(sm).astype(BF16))
        for h in range(nh):
            pv = _dot(probs[h], v_ref[pl.ds(off, blk), hs[h]])
            if diagonal:
                acc_scr[h] = pv
            elif online:
                acc_scr[h] = alphas[h] * acc_scr[h] + pv
            else:
                acc_scr[h] += pv

    block(i, True)

    def body(n, carry):
        block(i - 1 - n, False)
        return carry

    lax.fori_loop(0, i - lo, body, 0)
    pairs = []
    for k in range(nh // 2):
        even, odd = acc_scr[2 * k], acc_scr[2 * k + 1]
        num = jnp.where(lane < HEAD_DIM, even, odd)
        den = pltpu.roll(jnp.where(lane < HEAD_DIM, odd, even), HEAD_DIM, 1)
        pairs.append(num / den)
    o_ref[...] = jnp.concatenate(pairs, axis=1).astype(BF16)


def _attn(jlo, qp, ktp, vp, blk, nh, online):
    B, Lp, hw = qp.shape
    n_groups = hw // (nh * LANES)
    gw = nh * LANES
    grid_spec = pltpu.PrefetchScalarGridSpec(
        num_scalar_prefetch=1,
        grid=(B, n_groups, Lp // blk),
        in_specs=[
            pl.BlockSpec((None, blk, gw), lambda b, g, i, jlo_ref: (b, i, g)),
            pl.BlockSpec((None, gw, Lp), lambda b, g, i, jlo_ref: (b, g, 0)),
            pl.BlockSpec((None, Lp, gw), lambda b, g, i, jlo_ref: (b, 0, g)),
        ],
        out_specs=pl.BlockSpec((None, blk, gw // 2), lambda b, g, i, jlo_ref: (b, i, g)),
        scratch_shapes=[pltpu.VMEM((nh, blk, LANES), F32), pltpu.VMEM((nh, blk, 1), F32)],
    )
    return pl.pallas_call(
        functools.partial(_attn_kernel, blk=blk, nh=nh, online=online),
        grid_spec=grid_spec,
        out_shape=jax.ShapeDtypeStruct((B, Lp, hw // 2), BF16),
        compiler_params=_params(("parallel", "parallel", "arbitrary")),
        name="attn_online" if online else "attn",
    )(jlo, qp, ktp, vp)


def _first_live_block(f2, span, blk, n_heads, group):
    B, Lp, _ = f2.shape
    nq = Lp // blk
    f_first = f2[:, 0::blk, :n_heads]
    f_last = f2[:, blk - 1::blk, :n_heads]
    gap = f_first[:, :, None, :] - f_last[:, None, :, :]
    jj = jnp.arange(nq)
    live = (gap + span >= -SKIP_LOG2) | (jj[None, None, :, None] >= jj[None, :, None, None])
    jlo = jnp.argmax(live, axis=2).astype(jnp.int32)
    jlo = jnp.min(jlo.reshape(B, nq, n_heads // group, group), axis=3)
    return jnp.transpose(jlo, (0, 2, 1)).reshape(-1)


def _outmlp_kernel(h_ref, yrw_ref, yfx_ref, gate_ref, wrw_ref, wfx_ref, wo_ref, gm_ref, wup_ref,
                   wdn_ref, o_ref, *, ff_chunk):
    d = h_ref.shape[1]
    gates = gate_ref[...].astype(F32)
    merged = gates[:, :d] * _dot(yrw_ref[...], wrw_ref[...]) + gates[:, d:] * _dot(yfx_ref[...], wfx_ref[...])
    h1 = h_ref[...] + _dot(merged.astype(BF16), wo_ref[...])
    ms = jnp.mean(h1 * h1, axis=-1, keepdims=True)
    z = (h1 * lax.rsqrt(ms + NORM_EPS) * gm_ref[...]).astype(BF16)
    acc = h1
    for c0 in range(0, wup_ref.shape[1], ff_chunk):
        u = jnp.maximum(_dot(z, wup_ref[:, c0:c0 + ff_chunk]), 0.0)
        acc = acc + _dot((u * u).astype(BF16), wdn_ref[c0:c0 + ff_chunk, :])
    o_ref[...] = acc


def _outmlp(h2, yrw, yfx, gates, wrw, wfx, wo, gm, wup, wdn, tm):
    rows, d = h2.shape
    d_rw, d_fx, d_ff = wrw.shape[0], wfx.shape[0], wup.shape[1]
    return pl.pallas_call(
        functools.partial(_outmlp_kernel, ff_chunk=min(d_ff, 1024)),
        grid=(rows // tm,),
        in_specs=[
            pl.BlockSpec((tm, d), lambda i: (i, 0)),
            pl.BlockSpec((tm, d_rw), lambda i: (i, 0)),
            pl.BlockSpec((tm, d_fx), lambda i: (i, 0)),
            pl.BlockSpec((tm, 2 * d), lambda i: (i, 0)),
            _const_spec((d_rw, d)), _const_spec((d_fx, d)), _const_spec((d, d)),
            _const_spec((1, d)), _const_spec((d, d_ff)), _const_spec((d_ff, d)),
        ],
        out_specs=pl.BlockSpec((tm, d), lambda i: (i, 0)),
        out_shape=jax.ShapeDtypeStruct((rows, d), F32),
        compiler_params=_params(("parallel",)),
        name="outmlp",
    )(h2, yrw, yfx, gates, wrw, wfx, wo, gm, wup, wdn)


def _largest_divisor(n, candidates):
    for c in candidates:
        if n % c == 0:
            return c
    raise ValueError(f"no tile in {candidates} divides {n}")


def _attn_constants(n_heads, shift):
    hw = n_heads * LANES
    sq = np.zeros((LANES, hw), np.float32)
    sk = np.zeros((LANES, hw), np.float32)
    cq = np.zeros((1, hw), np.float32)
    cq_shift = np.zeros((1, hw), np.float32)
    ck = np.zeros((1, hw), np.float32)
    for h in range(n_heads):
        base = h * LANES + (HEAD_DIM if h % 2 == 0 else 0)
        for part in range(3):
            sq[part * n_heads + h, base + part] = 1.0
            sk[part * n_heads + h, base + 3 + part] = -1.0
            cq[0, base + 3 + part] = 1.0
            ck[0, base + part] = 1.0
        cq_shift[0, base + 6] = -1.0
        ck[0, base + 6] = 1.0
    return (jnp.asarray(sq, BF16), jnp.asarray(sk, BF16), jnp.asarray(cq) + shift * jnp.asarray(cq_shift),
            jnp.asarray(ck))


def kernel(x, meta, norm_mix, w_in, b_gate, b_f, tm_mu, w0, w_lora_up, a0, a_lora_up, g_lora_up, k_k, k_a,
           r_k, lnx_g, lnx_b, q_gain, k_gain, w_out_rw, w_out_fx, w_o, norm_mlp, w_up, w_down):
    B, S, D = x.shape
    depth = w_in.shape[0]
    d_rw = w0.shape[1]
    d_fx = w_out_fx.shape[1]
    h_fx = b_f.shape[1]
    c_rw = 3 * d_rw + LORA_W + LORA_A + LORA_G
    c_fx = 3 * d_fx + h_fx
    assert d_rw % LANES == 0 and h_fx % ATTN_GROUP == 0 and d_fx == h_fx * HEAD_DIM and 3 * h_fx <= LANES
    assert LORA_W + LORA_A == LANES and LORA_G == LANES and HEAD_DIM * 2 == LANES

    L = N_META + S
    Lp = -(-L // SEQ_ALIGN) * SEQ_ALIGN
    rows = B * Lp
    tm_row = _largest_divisor(rows, (512, 256, 128))
    tm_prep = _largest_divisor(Lp, (256, 128))
    blk = SEQ_ALIGN
    wkv_bb = _largest_divisor(B, (WKV_BATCH_ROWS, 1))

    meta_b = jnp.broadcast_to(meta.astype(x.dtype)[None], (B, N_META, D))
    h = jnp.concatenate([meta_b, x, jnp.zeros((B, Lp - L, D), x.dtype)], axis=1).reshape(rows, D)

    tri_wkv = jnp.asarray(np.tri(WKV_CHUNK), BF16)
    tri_prep = jnp.asarray(np.tri(tm_prep), BF16)
    seg = jnp.arange(LANES) // HEAD_DIM
    bd = (seg[:, None] == seg[None, :]).astype(BF16)
    zeros_lora = jnp.zeros((LORA_W, d_rw), F32)
    scale = HEAD_DIM ** -0.5
    row = lambda t: t.reshape(1, -1)

    for l in range(depth):
        wl = w_in[l]
        w_rw = wl[:, :c_rw].astype(BF16)
        w_fx = jnp.pad(wl[:, c_rw:c_rw + c_fx], ((0, 0), (0, LANES - h_fx))).astype(BF16)
        w_g = wl[:, c_rw + c_fx:].astype(BF16)

        u, feat_rw, gates = _inproj(h, row(norm_mix[l]), w_rw, w_g, row(b_gate[l]), tm_row)

        y_rw = _wkv(
            feat_rw.reshape(B, Lp, c_rw), row(tm_mu[l]), row(w0[l]),
            jnp.concatenate([w_lora_up[l], zeros_lora], axis=0).astype(BF16), row(a0[l]),
            jnp.concatenate([zeros_lora, a_lora_up[l]], axis=0).astype(BF16), g_lora_up[l].astype(BF16),
            row(k_k[l]), row(k_a[l]), row(r_k[l]), row(lnx_g[l]), row(lnx_b[l]), bd, tri_wkv, d_rw, wkv_bb)

        qg = jnp.tile(q_gain[l] * (scale * LOG2E), h_fx)[None, :]
        kg = jnp.tile(k_gain[l], h_fx)[None, :]
        bf = jnp.pad(b_f[l], (0, LANES - h_fx))[None, :]
        cmax = 1.02 * HEAD_DIM * jnp.max(jnp.abs(qg)) * jnp.max(jnp.abs(kg))
        shift = (1.01 * cmax).astype(BF16).astype(F32)
        sq, sk, cq, ck = _attn_constants(h_fx, shift)
        qp, ktp, vp, f2 = _fxprep(u.reshape(B, Lp, D), w_fx, qg, kg, bf, sq, sk, cq, ck, tri_prep, h_fx, tm_prep)
        y_fx = lax.cond(
            cmax <= FIXED_SHIFT_MAX_LOG2,
            lambda ops: _attn(_first_live_block(f2, 0.0, blk, h_fx, ATTN_GROUP), *ops, blk, ATTN_GROUP, False),
            lambda ops: _attn(_first_live_block(f2, 2.0 * cmax, blk, h_fx, ATTN_GROUP), *ops, blk, ATTN_GROUP, True),
            (qp, ktp, vp))

        h = _outmlp(h, y_rw.reshape(rows, d_rw), y_fx.reshape(rows, d_fx), gates,
                    w_out_rw[l].astype(BF16), w_out_fx[l].astype(BF16), w_o[l].astype(BF16),
                    row(norm_mlp[l]), w_up[l].astype(BF16), w_down[l].astype(BF16), tm_row)

    return h.reshape(B, Lp, D)[:, N_META:N_META + S]
```

```python
import functools

import jax
import jax.numpy as jnp
import numpy as np
from jax import lax
from jax.experimental import pallas as pl
from jax.experimental.pallas import tpu as pltpu

F32 = jnp.float32
BF16 = jnp.bfloat16

N_META = 16
HEAD_DIM = 64
LORA_W = 64
LORA_A = 64
LORA_G = 128
NORM_EPS = 1e-6
GN_EPS = 64e-5
DECAY_OFFSET = 0.5
LOG2E = 1.4426950408889634

LANES = 128
SEQ_ALIGN = 256
SKIP_LOG2 = 160.0
WKV_CHUNK = 128
WKV_CHUNKS_PER_STEP = 2
WKV_BATCH_ROWS = 2
ATTN_GROUP = 8
FIXED_SHIFT_MAX_LOG2 = 50.0
VMEM_LIMIT_BYTES = 56 * 1024 * 1024
NEG_BIG = -1e30


def _dot(a, b):
    return jnp.dot(a, b, preferred_element_type=F32)


def _dot_nt(a, b):
    return lax.dot_general(a, b, (((1,), (1,)), ((), ())), preferred_element_type=F32)


def _split2(x):
    hi = x.astype(BF16)
    lo = (x - hi.astype(F32)).astype(BF16)
    return hi, lo


def _split3(x):
    hi = x.astype(BF16)
    r1 = x - hi.astype(F32)
    mid = r1.astype(BF16)
    lo = (r1 - mid.astype(F32)).astype(BF16)
    return hi, mid, lo


def _dot_exact_lhs(m_bf16, x):
    hi, mid, lo = _split3(x)
    return _dot(m_bf16, hi) + _dot(m_bf16, mid) + _dot(m_bf16, lo)


def _sigmoid(x):
    return 1.0 / (1.0 + jnp.exp(-x))


def _softplus(x):
    return jnp.maximum(x, 0.0) + jnp.log(1.0 + jnp.exp(-jnp.abs(x)))


def _const_spec(shape):
    nd = len(shape)
    return pl.BlockSpec(shape, lambda *_: (0,) * nd, pipeline_mode=pl.Buffered(1))


def _params(sem):
    return pltpu.CompilerParams(dimension_semantics=sem, vmem_limit_bytes=VMEM_LIMIT_BYTES)


def _rmsnorm_bf16(x, g):
    ms = jnp.mean(x * x, axis=-1, keepdims=True)
    return (x * lax.rsqrt(ms + NORM_EPS) * g).astype(BF16)


def _inproj_kernel(h_ref, g_ref, wrw_ref, wg_ref, bg_ref, u_ref, rw_ref, gate_ref):
    u = _rmsnorm_bf16(h_ref[...], g_ref[...])
    u_ref[...] = u
    rw_ref[...] = _dot(u, wrw_ref[...])
    gate_ref[...] = _sigmoid(_dot(u, wg_ref[...]) + bg_ref[...]).astype(BF16)


def _inproj(h2, g, wrw, wg, bg, tm):
    rows, d = h2.shape
    n_rw, n_g = wrw.shape[1], wg.shape[1]
    return pl.pallas_call(
        _inproj_kernel,
        grid=(rows // tm,),
        in_specs=[
            pl.BlockSpec((tm, d), lambda i: (i, 0)),
            _const_spec((1, d)),
            _const_spec((d, n_rw)),
            _const_spec((d, n_g)),
            _const_spec((1, n_g)),
        ],
        out_specs=[
            pl.BlockSpec((tm, d), lambda i: (i, 0)),
            pl.BlockSpec((tm, n_rw), lambda i: (i, 0)),
            pl.BlockSpec((tm, n_g), lambda i: (i, 0)),
        ],
        out_shape=[
            jax.ShapeDtypeStruct((rows, d), BF16),
            jax.ShapeDtypeStruct((rows, n_rw), F32),
            jax.ShapeDtypeStruct((rows, n_g), BF16),
        ],
        compiler_params=_params(("parallel",)),
        name="inproj",
    )(h2, g, wrw, wg, bg)


def _wkv_kernel(x_ref, mu_ref, w0_ref, wlw_ref, a0_ref, wla_ref, wlg_ref, kk_ref, ka_ref, rk_ref,
                lng_ref, lnb_ref, bd_ref, tri_ref, o_ref, m_ref, prev_ref, *, d_rw, bb, nck):
    T = WKV_CHUNK
    c = pl.program_id(1)
    n_pairs = d_rw // LANES
    c_rw = x_ref.shape[2]

    @pl.when(c == 0)
    def _():
        m_ref[...] = jnp.zeros_like(m_ref)
        prev_ref[...] = jnp.zeros_like(prev_ref)

    R = nck * T
    x = x_ref[...].reshape(bb * R, c_rw)
    row = lax.broadcasted_iota(jnp.int32, x.shape, 0)
    shifted = pltpu.roll(x, 1, 0)
    for bi in range(bb):
        shifted = jnp.where(row == bi * R, prev_ref[bi], shifted)
        prev_ref[bi] = x[(bi + 1) * R - 1:(bi + 1) * R, :]
    f = x + mu_ref[...] * (shifted - x)

    r = f[:, 0:d_rw]
    k = f[:, d_rw:2 * d_rw]
    v = f[:, 2 * d_rw:3 * d_rw]
    wa_d = f[:, 3 * d_rw:3 * d_rw + LORA_W + LORA_A]
    g_d = f[:, 3 * d_rw + LORA_W + LORA_A:]

    bd = bd_ref[...]

    def segsum(t):
        stacked = jnp.concatenate([t[:, p * LANES:(p + 1) * LANES] for p in range(n_pairs)], axis=0)
        out = _dot(stacked.astype(BF16), bd)
        rows = t.shape[0]
        return jnp.concatenate([out[p * rows:(p + 1) * rows] for p in range(n_pairs)], axis=1)

    w_log = -_softplus(-(w0_ref[...] + _dot(jnp.tanh(wa_d).astype(BF16), wlw_ref[...]))) - DECAY_OFFSET
    lw = -jnp.exp(w_log)
    a_lr = _sigmoid(a0_ref[...] + _dot(wa_d.astype(BF16), wla_ref[...]))
    g = _dot(_sigmoid(g_d).astype(BF16), wlg_ref[...])
    kk = k * kk_ref[...]
    k2 = k * (1.0 + (a_lr - 1.0) * ka_ref[...])
    sums = segsum(jnp.concatenate([kk * kk, r * k2 * rk_ref[...]], axis=0))
    kk = kk * lax.rsqrt(jnp.maximum(sums[:bb * R], 1e-24))
    bonus = sums[bb * R:] * v
    a_vec = -kk
    b_vec = kk * a_lr

    ri = lax.broadcasted_iota(jnp.int32, (T, T), 0)
    ci = lax.broadcasted_iota(jnp.int32, (T, T), 1)
    strict = ci < ri
    incl = ci <= ri
    eye = ci == ri
    tri_incl_bf = tri_ref[...]
    rc_xor = jnp.bitwise_xor(ri, ci)
    level_masks = []
    n = 1
    while n < T:
        level_masks.append(strict & (rc_xor >= n) & (rc_xor < 2 * n))
        n *= 2
    lane = lax.broadcasted_iota(jnp.int32, (T, LANES), 1)
    head0 = lane < HEAD_DIM
    pr = lax.broadcasted_iota(jnp.int32, (LANES, LANES), 0)
    pc = lax.broadcasted_iota(jnp.int32, (LANES, LANES), 1)
    same_head = (pr < HEAD_DIM) == (pc < HEAD_DIM)
    eye_p = pr == pc

    units = [(bi, ch, p) for bi in range(bb) for ch in range(nck) for p in range(n_pairs)]

    def tile(t, u):
        bi, ch, p = units[u]
        r0 = bi * R + ch * T
        return t[r0:r0 + T, p * LANES:(p + 1) * LANES]

    nu = range(len(units))
    heads = [(u, hh) for u in nu for hh in range(2)]
    zero = jnp.zeros((T, LANES), F32)

    lw_hi, lw_lo = _split2(lw)
    rt, vb, atb, dend_t, dt_row, sc = [], [], [], [], [], []
    for u in nu:
        lwu = tile(lw, u)
        clu = _dot(tri_incl_bf, tile(lw_hi, u)) + _dot(tri_incl_bf, tile(lw_lo, u))
        cl_last = clu[T - 1:T, :]
        dinv = jnp.exp(-clu)
        dend = jnp.exp(cl_last - clu)
        at = tile(a_vec, u) * jnp.exp(clu - lwu)
        bt = tile(b_vec, u) * dinv
        kt = tile(k2, u) * dinv
        rtu = tile(r, u) * jnp.exp(clu)
        rt.append(rtu)
        vb.append(tile(v, u).astype(BF16))
        atb.append(at.astype(BF16))
        dend_t.append(jnp.concatenate([(tile(b_vec, u) * dend).T, (tile(k2, u) * dend).T], axis=1).astype(BF16))
        dt_row.append(jnp.exp(cl_last))
        lhs = jnp.concatenate([jnp.where(head0, at, zero), jnp.where(head0, zero, at),
                               jnp.where(head0, rtu, zero), jnp.where(head0, zero, rtu)], axis=0).astype(BF16)
        rhs = jnp.concatenate([bt, kt], axis=0).astype(BF16)
        sc.append(_dot_nt(lhs, rhs))

    aab, aak, arbk = {}, {}, {}
    for (u, hh) in heads:
        aab[u, hh] = jnp.where(strict, sc[u][hh * T:(hh + 1) * T, 0:T], 0.0)
        aak[u, hh] = jnp.where(strict, sc[u][hh * T:(hh + 1) * T, T:2 * T], 0.0).astype(BF16)
        arbk[u, hh] = jnp.where(jnp.concatenate([incl, incl], axis=1), sc[u][(2 + hh) * T:(3 + hh) * T, :],
                                0.0).astype(BF16)

    tinv = {hd: jnp.where(eye, 1.0, jnp.where(level_masks[0], aab[hd], 0.0)).astype(BF16) for hd in heads}
    for lm in level_masks[1:]:
        half = {hd: _dot(tinv[hd], jnp.where(lm, aab[hd], 0.0).astype(BF16)).astype(BF16) for hd in heads}
        tinv = {hd: (tinv[hd].astype(F32) + _dot(half[hd], tinv[hd])).astype(BF16) for hd in heads}

    aakv = {hd: _dot(aak[hd], vb[hd[0]]).astype(BF16) for hd in heads}
    ap_h = {hd: _dot(tinv[hd], atb[hd[0]]) for hd in heads}
    w2_h = {hd: _dot(tinv[hd], aakv[hd]) for hd in heads}
    apn = [jnp.where(head0, ap_h[u, 0], ap_h[u, 1]).astype(BF16) for u in nu]
    w2v = [jnp.concatenate([jnp.where(head0, w2_h[u, 0], w2_h[u, 1]).astype(BF16), vb[u]], axis=0) for u in nu]
    qp_h = {hd: _dot(arbk[hd][:, 0:T], apn[hd[0]]) for hd in heads}
    yl_h = {hd: _dot(arbk[hd], w2v[hd[0]]) for hd in heads}

    ys = []
    for u in nu:
        qp = rt[u] + jnp.where(head0, qp_h[u, 0], qp_h[u, 1])
        yl = jnp.where(head0, yl_h[u, 0], yl_h[u, 1])
        g_mat = jnp.where(same_head, _dot(dend_t[u][:, 0:T], apn[u]), 0.0) + jnp.where(eye_p, dt_row[u], 0.0)
        c_mat = jnp.where(same_head, _dot(dend_t[u], w2v[u]), 0.0)
        state = units[u][0] * n_pairs + units[u][2]
        m0 = m_ref[state].astype(BF16)
        ys.append(_dot(qp.astype(BF16), m0) + yl)
        m_ref[state] = _dot(g_mat.astype(BF16), m0) + c_mat

    y = jnp.concatenate([jnp.concatenate(ys[g * n_pairs:(g + 1) * n_pairs], axis=1) for g in range(bb * nck)],
                        axis=0)
    inv_n = 1.0 / HEAD_DIM
    mean = segsum(y) * inv_n
    yc = y - mean
    var = segsum(yc * yc) * inv_n
    yn = yc * lax.rsqrt(var + GN_EPS) * lng_ref[...] + lnb_ref[...]
    o_ref[...] = ((yn + bonus) * g).astype(BF16).reshape(bb, R, d_rw)


def _wkv(feat_rw, mu, w0, wlw, a0, wla, wlg, k_k, k_a, r_k, lnx_g, lnx_b, bd, tri, d_rw, bb, nck):
    B, Lp, c_rw = feat_rw.shape
    T = WKV_CHUNK
    vec = _const_spec((1, d_rw))
    return pl.pallas_call(
        functools.partial(_wkv_kernel, d_rw=d_rw, bb=bb, nck=nck),
        grid=(B // bb, Lp // (nck * T)),
        in_specs=[
            pl.BlockSpec((bb, nck * T, c_rw), lambda b, c: (b, c, 0)),
            _const_spec((1, c_rw)),
            vec, _const_spec((LANES, d_rw)), vec, _const_spec((LANES, d_rw)), _const_spec((LORA_G, d_rw)),
            vec, vec, vec, vec, vec,
            _const_spec((LANES, LANES)), _const_spec((T, T)),
        ],
        out_specs=pl.BlockSpec((bb, nck * T, d_rw), lambda b, c: (b, c, 0)),
        out_shape=jax.ShapeDtypeStruct((B, Lp, d_rw), BF16),
        scratch_shapes=[pltpu.VMEM((bb * (d_rw // LANES), LANES, LANES), F32),
                        pltpu.VMEM((bb, 1, c_rw), F32)],
        compiler_params=_params(("parallel", "arbitrary")),
        name="wkv",
    )(feat_rw, mu, w0, wlw, a0, wla, wlg, k_k, k_a, r_k, lnx_g, lnx_b, bd, tri)


def _fxprep_kernel(u_ref, wfx_ref, qg_ref, kg_ref, bf_ref, sq_ref, sk_ref, cq_ref, ck_ref, tri_ref,
                   q_ref, kt_ref, v_ref, f2_ref, carry_ref, *, n_heads):
    i = pl.program_id(1)

    @pl.when(i == 0)
    def _():
        carry_ref[...] = jnp.zeros_like(carry_ref)

    d_fx = n_heads * HEAD_DIM
    x = _dot(u_ref[...], wfx_ref[...])
    tm = x.shape[0]
    fl = x[:, 3 * d_fx:3 * d_fx + LANES]
    z = fl + bf_ref[...]
    log_f = jnp.minimum(z, 0.0) - jnp.log(1.0 + jnp.exp(-jnp.abs(z)))
    lane = lax.broadcasted_iota(jnp.int32, (tm, LANES), 1)
    log_f = jnp.where(lane < n_heads, log_f, 0.0)
    fcum = carry_ref[...] + _dot_exact_lhs(tri_ref[...], log_f)
    carry_ref[...] = fcum[tm - 1:tm, :]
    f2 = fcum * LOG2E
    f2_ref[...] = f2
    hi, mid, lo = _split3(f2)
    parts = (hi.astype(F32) + pltpu.roll(mid.astype(F32), n_heads, 1)
             + pltpu.roll(lo.astype(F32), 2 * n_heads, 1)).astype(BF16)
    extra_q = _dot(parts, sq_ref[...]) + cq_ref[...]
    extra_k = _dot(parts, sk_ref[...]) + ck_ref[...]

    low = lane < HEAD_DIM
    inv_n = 1.0 / HEAD_DIM

    def pair_norm(p, gain):
        p2 = p * p
        ms_lo = jnp.sum(jnp.where(low, p2, 0.0), axis=-1, keepdims=True) * inv_n
        ms_hi = jnp.sum(jnp.where(low, 0.0, p2), axis=-1, keepdims=True) * inv_n
        return p * jnp.where(low, lax.rsqrt(ms_lo + NORM_EPS), lax.rsqrt(ms_hi + NORM_EPS)) * gain

    for kp in range(n_heads // 2):
        ps = slice(kp * LANES, (kp + 1) * LANES)
        te = slice(2 * kp * LANES, (2 * kp + 1) * LANES)
        to = slice((2 * kp + 1) * LANES, (2 * kp + 2) * LANES)
        qn = pair_norm(x[:, ps], qg_ref[:, ps])
        kn = pair_norm(x[:, d_fx + kp * LANES:d_fx + (kp + 1) * LANES], kg_ref[:, ps])
        vp = x[:, 2 * d_fx + kp * LANES:2 * d_fx + (kp + 1) * LANES]
        q_ref[:, te] = jnp.where(low, qn, extra_q[:, te]).astype(BF16)
        q_ref[:, to] = jnp.where(low, extra_q[:, to], qn).astype(BF16)
        kt_ref[te, :] = jnp.where(low, kn, extra_k[:, te]).T.astype(BF16)
        kt_ref[to, :] = jnp.where(low, extra_k[:, to], kn).T.astype(BF16)
        v_ref[:, te] = jnp.where(low, vp, 1.0).astype(BF16)
        v_ref[:, to] = jnp.where(low, 1.0, vp).astype(BF16)


def _fxprep(u3, wfx, qg, kg, bf, sq, sk, cq, ck, tri, n_heads, tm):
    B, Lp, d = u3.shape
    n_fx = wfx.shape[1]
    d_fx = n_heads * HEAD_DIM
    hw = n_heads * LANES
    return pl.pallas_call(
        functools.partial(_fxprep_kernel, n_heads=n_heads),
        grid=(B, Lp // tm),
        in_specs=[
            pl.BlockSpec((None, tm, d), lambda b, i: (b, i, 0)),
            _const_spec((d, n_fx)),
            _const_spec((1, d_fx)), _const_spec((1, d_fx)), _const_spec((1, LANES)),
            _const_spec((LANES, hw)), _const_spec((LANES, hw)),
            _const_spec((1, hw)), _const_spec((1, hw)), _const_spec((tm, tm)),
        ],
        out_specs=[
            pl.BlockSpec((None, tm, hw), lambda b, i: (b, i, 0)),
            pl.BlockSpec((None, hw, tm), lambda b, i: (b, 0, i)),
            pl.BlockSpec((None, tm, hw), lambda b, i: (b, i, 0)),
            pl.BlockSpec((None, tm, LANES), lambda b, i: (b, i, 0)),
        ],
        out_shape=[
            jax.ShapeDtypeStruct((B, Lp, hw), BF16),
            jax.ShapeDtypeStruct((B, hw, Lp), BF16),
            jax.ShapeDtypeStruct((B, Lp, hw), BF16),
            jax.ShapeDtypeStruct((B, Lp, LANES), F32),
        ],
        scratch_shapes=[pltpu.VMEM((1, LANES), F32)],
        compiler_params=_params(("parallel", "arbitrary")),
        name="fxprep",
    )(u3, wfx, qg, kg, bf, sq, sk, cq, ck, tri)


def _attn_kernel(jlo_ref, q_ref, kt_ref, v_ref, o_ref, acc_scr, m_scr, *, blk, nh, online):
    b, g, i = pl.program_id(0), pl.program_id(1), pl.program_id(2)
    lo = jlo_ref[(b * pl.num_programs(1) + g) * pl.num_programs(2) + i]
    ri = lax.broadcasted_iota(jnp.int32, (blk, blk), 0)
    ci = lax.broadcasted_iota(jnp.int32, (blk, blk), 1)
    causal = ci <= ri
    lane = lax.broadcasted_iota(jnp.int32, (blk, LANES), 1)
    hs = [slice(h * LANES, (h + 1) * LANES) for h in range(nh)]

    def block(j, diagonal):
        off = pl.multiple_of(j * blk, blk)
        s = [_dot(q_ref[:, hs[h]], kt_ref[hs[h], pl.ds(off, blk)]) for h in range(nh)]
        probs, alphas = [], []
        for h in range(nh):
            sm = jnp.where(causal, s[h], NEG_BIG) if diagonal else s[h]
            if online:
                row_max = jnp.max(sm, axis=-1, keepdims=True)
                if diagonal:
                    m_new = row_max
                else:
                    m_old = m_scr[h]
                    m_new = jnp.maximum(m_old, row_max)
                    alphas.append(jnp.exp2(m_old - m_new))
                m_scr[h] = m_new
                sm = sm - m_new
            probs.append(jnp.exp2(sm).astype(BF16))
        for h in range(nh):
            pv = _dot(probs[h], v_ref[pl.ds(off, blk), hs[h]])
            if diagonal:
                acc_scr[h] = pv
            elif online:
                acc_scr[h] = alphas[h] * acc_scr[h] + pv
            else:
                acc_scr[h] += pv

    block(i, True)

    def body(n, carry):
        block(i - 1 - n, False)
        return carry

    lax.fori_loop(0, i - lo, body, 0)
    pairs = []
    for k in range(nh // 2):
        even, odd = acc_scr[2 * k], acc_scr[2 * k + 1]
        num = jnp.where(lane < HEAD_DIM, even, odd)
        den = pltpu.roll(jnp.where(lane < HEAD_DIM, odd, even), HEAD_DIM, 1)
        pairs.append(num / den)
    o_ref[...] = jnp.concatenate(pairs, axis=1).astype(BF16)


def _attn(jlo, qp, ktp, vp, blk, nh, online):
    B, Lp, hw = qp.shape
    n_groups = hw // (nh * LANES)
    gw = nh * LANES
    grid_spec = pltpu.PrefetchScalarGridSpec(
        num_scalar_prefetch=1,
        grid=(B, n_groups, Lp // blk),
        in_specs=[
            pl.BlockSpec((None, blk, gw), lambda b, g, i, jlo_ref: (b, i, g)),
            pl.BlockSpec((None, gw, Lp), lambda b, g, i, jlo_ref: (b, g, 0), pipeline_mode=pl.Buffered(1)),
            pl.BlockSpec((None, Lp, gw), lambda b, g, i, jlo_ref: (b, 0, g), pipeline_mode=pl.Buffered(1)),
        ],
        out_specs=pl.BlockSpec((None, blk, gw // 2), lambda b, g, i, jlo_ref: (b, i, g)),
        scratch_shapes=[pltpu.VMEM((nh, blk, LANES), F32), pltpu.VMEM((nh, blk, 1), F32)],
    )
    return pl.pallas_call(
        functools.partial(_attn_kernel, blk=blk, nh=nh, online=online),
        grid_spec=grid_spec,
        out_shape=jax.ShapeDtypeStruct((B, Lp, hw // 2), BF16),
        compiler_params=_params(("parallel", "parallel", "arbitrary")),
        name="attn_online" if online else "attn",
    )(jlo, qp, ktp, vp)


def _first_live_block(f2, span, blk, n_heads, group):
    B, Lp, _ = f2.shape
    nq = Lp // blk
    f_first = f2[:, 0::blk, :n_heads]
    f_last = f2[:, blk - 1::blk, :n_heads]
    gap = f_first[:, :, None, :] - f_last[:, None, :, :]
    jj = jnp.arange(nq)
    live = (gap + span >= -SKIP_LOG2) | (jj[None, None, :, None] >= jj[None, :, None, None])
    jlo = jnp.argmax(live, axis=2).astype(jnp.int32)
    jlo = jnp.min(jlo.reshape(B, nq, n_heads // group, group), axis=3)
    return jnp.transpose(jlo, (0, 2, 1)).reshape(-1)


def _outmlp_kernel(h_ref, yrw_ref, yfx_ref, gate_ref, wrw_ref, wfx_ref, wo_ref, gm_ref, wup_ref,
                   wdn_ref, o_ref, *, ff_chunk):
    d = h_ref.shape[1]
    gates = gate_ref[...].astype(F32)
    merged = gates[:, :d] * _dot(yrw_ref[...], wrw_ref[...]) + gates[:, d:] * _dot(yfx_ref[...], wfx_ref[...])
    h1 = h_ref[...] + _dot(merged.astype(BF16), wo_ref[...])
    ms = jnp.mean(h1 * h1, axis=-1, keepdims=True)
    z = (h1 * lax.rsqrt(ms + NORM_EPS) * gm_ref[...]).astype(BF16)
    acc = h1
    for c0 in range(0, wup_ref.shape[1], ff_chunk):
        u = jnp.maximum(_dot(z, wup_ref[:, c0:c0 + ff_chunk]), 0.0)
        acc = acc + _dot((u * u).astype(BF16), wdn_ref[c0:c0 + ff_chunk, :])
    o_ref[...] = acc


def _outmlp(h2, yrw, yfx, gates, wrw, wfx, wo, gm, wup, wdn, tm):
    rows, d = h2.shape
    d_rw, d_fx, d_ff = wrw.shape[0], wfx.shape[0], wup.shape[1]
    return pl.pallas_call(
        functools.partial(_outmlp_kernel, ff_chunk=min(d_ff, 1024)),
        grid=(rows // tm,),
        in_specs=[
            pl.BlockSpec((tm, d), lambda i: (i, 0)),
            pl.BlockSpec((tm, d_rw), lambda i: (i, 0)),
            pl.BlockSpec((tm, d_fx), lambda i: (i, 0)),
            pl.BlockSpec((tm, 2 * d), lambda i: (i, 0)),
            _const_spec((d_rw, d)), _const_spec((d_fx, d)), _const_spec((d, d)),
            _const_spec((1, d)), _const_spec((d, d_ff)), _const_spec((d_ff, d)),
        ],
        out_specs=pl.BlockSpec((tm, d), lambda i: (i, 0)),
        out_shape=jax.ShapeDtypeStruct((rows, d), F32),
        compiler_params=_params(("parallel",)),
        name="outmlp",
    )(h2, yrw, yfx, gates, wrw, wfx, wo, gm, wup, wdn)


def _largest_divisor(n, candidates):
    for c in candidates:
        if n % c == 0:
            return c
    raise ValueError(f"no tile in {candidates} divides {n}")


def _attn_constants(n_heads, shift):
    hw = n_heads * LANES
    sq = np.zeros((LANES, hw), np.float32)
    sk = np.zeros((LANES, hw), np.float32)
    cq = np.zeros((1, hw), np.float32)
    cq_shift = np.zeros((1, hw), np.float32)
    ck = np.zeros((1, hw), np.float32)
    for h in range(n_heads):
        base = h * LANES + (HEAD_DIM if h % 2 == 0 else 0)
        for part in range(3):
            sq[part * n_heads + h, base + part] = 1.0
            sk[part * n_heads + h, base + 3 + part] = -1.0
            cq[0, base + 3 + part] = 1.0
            ck[0, base + part] = 1.0
        cq_shift[0, base + 6] = -1.0
        ck[0, base + 6] = 1.0
    return (jnp.asarray(sq, BF16), jnp.asarray(sk, BF16), jnp.asarray(cq) + shift * jnp.asarray(cq_shift),
            jnp.asarray(ck))


def kernel(x, meta, norm_mix, w_in, b_gate, b_f, tm_mu, w0, w_lora_up, a0, a_lora_up, g_lora_up, k_k, k_a,
           r_k, lnx_g, lnx_b, q_gain, k_gain, w_out_rw, w_out_fx, w_o, norm_mlp, w_up, w_down):
    B, S, D = x.shape
    depth = w_in.shape[0]
    d_rw = w0.shape[1]
    d_fx = w_out_fx.shape[1]
    h_fx = b_f.shape[1]
    c_rw = 3 * d_rw + LORA_W + LORA_A + LORA_G
    c_fx = 3 * d_fx + h_fx
    assert d_rw % LANES == 0 and h_fx % ATTN_GROUP == 0 and d_fx == h_fx * HEAD_DIM and 3 * h_fx <= LANES
    assert LORA_W + LORA_A == LANES and LORA_G == LANES and HEAD_DIM * 2 == LANES

    L = N_META + S
    Lp = -(-L // SEQ_ALIGN) * SEQ_ALIGN
    rows = B * Lp
    tm_row = _largest_divisor(rows, (512, 256, 128))
    tm_prep = _largest_divisor(Lp, (256, 128))
    blk = SEQ_ALIGN
    wkv_bb = _largest_divisor(B, (WKV_BATCH_ROWS, 1))
    wkv_nck = _largest_divisor(Lp // WKV_CHUNK, (WKV_CHUNKS_PER_STEP, 1))

    meta_b = jnp.broadcast_to(meta.astype(x.dtype)[None], (B, N_META, D))
    h = jnp.concatenate([meta_b, x, jnp.zeros((B, Lp - L, D), x.dtype)], axis=1).reshape(rows, D)

    tri_wkv = jnp.asarray(np.tri(WKV_CHUNK), BF16)
    tri_prep = jnp.asarray(np.tri(tm_prep), BF16)
    seg = jnp.arange(LANES) // HEAD_DIM
    bd = (seg[:, None] == seg[None, :]).astype(BF16)
    zeros_lora = jnp.zeros((LORA_W, d_rw), F32)
    scale = HEAD_DIM ** -0.5
    row = lambda t: t.reshape(1, -1)

    for l in range(depth):
        wl = w_in[l]
        w_rw = wl[:, :c_rw].astype(BF16)
        w_fx = jnp.pad(wl[:, c_rw:c_rw + c_fx], ((0, 0), (0, LANES - h_fx))).astype(BF16)
        w_g = wl[:, c_rw + c_fx:].astype(BF16)

        u, feat_rw, gates = _inproj(h, row(norm_mix[l]), w_rw, w_g, row(b_gate[l]), tm_row)

        y_rw = _wkv(
            feat_rw.reshape(B, Lp, c_rw), row(tm_mu[l]), row(w0[l]),
            jnp.concatenate([w_lora_up[l], zeros_lora], axis=0).astype(BF16), row(a0[l]),
            jnp.concatenate([zeros_lora, a_lora_up[l]], axis=0).astype(BF16), g_lora_up[l].astype(BF16),
            row(k_k[l]), row(k_a[l]), row(r_k[l]), row(lnx_g[l]), row(lnx_b[l]), bd, tri_wkv, d_rw, wkv_bb, wkv_nck)

        qg = jnp.tile(q_gain[l] * (scale * LOG2E), h_fx)[None, :]
        kg = jnp.tile(k_gain[l], h_fx)[None, :]
        bf = jnp.pad(b_f[l], (0, LANES - h_fx))[None, :]
        cmax = 1.02 * HEAD_DIM * jnp.max(jnp.abs(qg)) * jnp.max(jnp.abs(kg))
        shift = (1.01 * cmax).astype(BF16).astype(F32)
        sq, sk, cq, ck = _attn_constants(h_fx, shift)
        qp, ktp, vp, f2 = _fxprep(u.reshape(B, Lp, D), w_fx, qg, kg, bf, sq, sk, cq, ck, tri_prep, h_fx, tm_prep)
        y_fx = lax.cond(
            cmax <= FIXED_SHIFT_MAX_LOG2,
            lambda ops: _attn(_first_live_block(f2, 0.0, blk, h_fx, ATTN_GROUP), *ops, blk, ATTN_GROUP, False),
            lambda ops: _attn(_first_live_block(f2, 2.0 * cmax, blk, h_fx, ATTN_GROUP), *ops, blk, ATTN_GROUP, True),
            (qp, ktp, vp))

        h = _outmlp(h, y_rw.reshape(rows, d_rw), y_fx.reshape(rows, d_fx), gates,
                    w_out_rw[l].astype(BF16), w_out_fx[l].astype(BF16), w_o[l].astype(BF16),
                    row(norm_mlp[l]), w_up[l].astype(BF16), w_down[l].astype(BF16), tm_row)

    return h.reshape(B, Lp, D)[:, N_META:N_META + S]
```

```python
import functools

import jax
import jax.numpy as jnp
import numpy as np
from jax import lax
from jax.experimental import pallas as pl
from jax.experimental.pallas import tpu as pltpu

F32 = jnp.float32
BF16 = jnp.bfloat16

N_META = 16
HEAD_DIM = 64
LORA_W = 64
LORA_A = 64
LORA_G = 128
NORM_EPS = 1e-6
GN_EPS = 64e-5
DECAY_OFFSET = 0.5
LOG2E = 1.4426950408889634

LANES = 128
SEQ_ALIGN = 256
SKIP_LOG2 = 160.0
WKV_CHUNK = 128
WKV_CHUNKS_PER_STEP = 3
WKV_BATCH_ROWS = 2
ATTN_GROUP = 8
FIXED_SHIFT_MAX_LOG2 = 50.0
VMEM_LIMIT_BYTES = 56 * 1024 * 1024
NEG_BIG = -1e30


def _dot(a, b):
    return jnp.dot(a, b, preferred_element_type=F32)


def _dot_nt(a, b):
    return lax.dot_general(a, b, (((1,), (1,)), ((), ())), preferred_element_type=F32)


def _split2(x):
    hi = x.astype(BF16)
    lo = (x - hi.astype(F32)).astype(BF16)
    return hi, lo


def _split3(x):
    hi = x.astype(BF16)
    r1 = x - hi.astype(F32)
    mid = r1.astype(BF16)
    lo = (r1 - mid.astype(F32)).astype(BF16)
    return hi, mid, lo


def _dot_exact_lhs(m_bf16, x):
    hi, mid, lo = _split3(x)
    return _dot(m_bf16, hi) + _dot(m_bf16, mid) + _dot(m_bf16, lo)


def _sigmoid(x):
    return 1.0 / (1.0 + jnp.exp(-x))


def _softplus(x):
    return jnp.maximum(x, 0.0) + jnp.log(1.0 + jnp.exp(-jnp.abs(x)))


def _const_spec(shape):
    nd = len(shape)
    return pl.BlockSpec(shape, lambda *_: (0,) * nd, pipeline_mode=pl.Buffered(1))


def _params(sem):
    return pltpu.CompilerParams(dimension_semantics=sem, vmem_limit_bytes=VMEM_LIMIT_BYTES)


def _rmsnorm_bf16(x, g):
    ms = jnp.mean(x * x, axis=-1, keepdims=True)
    return (x * lax.rsqrt(ms + NORM_EPS) * g).astype(BF16)


def _wkv_kernel(x_ref, mu_ref, w0_ref, wlw_ref, a0_ref, wla_ref, wlg_ref, kk_ref, ka_ref, rk_ref,
                lng_ref, lnb_ref, bd_ref, tri_ref, o_ref, m_ref, prev_ref, *, d_rw, bb, nck):
    T = WKV_CHUNK
    c = pl.program_id(1)
    n_pairs = d_rw // LANES
    c_rw = x_ref.shape[2]

    @pl.when(c == 0)
    def _():
        m_ref[...] = jnp.zeros_like(m_ref)
        prev_ref[...] = jnp.zeros_like(prev_ref)

    R = nck * T
    x = x_ref[...].reshape(bb * R, c_rw)
    row = lax.broadcasted_iota(jnp.int32, x.shape, 0)
    shifted = pltpu.roll(x, 1, 0)
    for bi in range(bb):
        shifted = jnp.where(row == bi * R, prev_ref[bi], shifted)
        prev_ref[bi] = x[(bi + 1) * R - 1:(bi + 1) * R, :]
    f = x + mu_ref[...] * (shifted - x)

    r = f[:, 0:d_rw]
    k = f[:, d_rw:2 * d_rw]
    v = f[:, 2 * d_rw:3 * d_rw]
    wa_d = f[:, 3 * d_rw:3 * d_rw + LORA_W + LORA_A]
    g_d = f[:, 3 * d_rw + LORA_W + LORA_A:]

    bd = bd_ref[...]

    def segsum(t):
        stacked = jnp.concatenate([t[:, p * LANES:(p + 1) * LANES] for p in range(n_pairs)], axis=0)
        out = _dot(stacked.astype(BF16), bd)
        rows = t.shape[0]
        return jnp.concatenate([out[p * rows:(p + 1) * rows] for p in range(n_pairs)], axis=1)

    w_log = -_softplus(-(w0_ref[...] + _dot(jnp.tanh(wa_d).astype(BF16), wlw_ref[...]))) - DECAY_OFFSET
    lw = -jnp.exp(w_log)
    a_lr = _sigmoid(a0_ref[...] + _dot(wa_d.astype(BF16), wla_ref[...]))
    g = _dot(_sigmoid(g_d).astype(BF16), wlg_ref[...])
    kk = k * kk_ref[...]
    k2 = k * (1.0 + (a_lr - 1.0) * ka_ref[...])
    sums = segsum(jnp.concatenate([kk * kk, r * k2 * rk_ref[...]], axis=0))
    kk = kk * lax.rsqrt(jnp.maximum(sums[:bb * R], 1e-24))
    bonus = sums[bb * R:] * v
    a_vec = -kk
    b_vec = kk * a_lr

    ri = lax.broadcasted_iota(jnp.int32, (T, T), 0)
    ci = lax.broadcasted_iota(jnp.int32, (T, T), 1)
    strict = ci < ri
    incl = ci <= ri
    eye = ci == ri
    tri_incl_bf = tri_ref[...]
    rc_xor = jnp.bitwise_xor(ri, ci)
    level_masks = []
    n = 1
    while n < T:
        level_masks.append(strict & (rc_xor >= n) & (rc_xor < 2 * n))
        n *= 2
    lane = lax.broadcasted_iota(jnp.int32, (T, LANES), 1)
    head0 = lane < HEAD_DIM
    pr = lax.broadcasted_iota(jnp.int32, (LANES, LANES), 0)
    pc = lax.broadcasted_iota(jnp.int32, (LANES, LANES), 1)
    same_head = (pr < HEAD_DIM) == (pc < HEAD_DIM)
    eye_p = pr == pc

    units = [(bi, ch, p) for bi in range(bb) for ch in range(nck) for p in range(n_pairs)]

    def tile(t, u):
        bi, ch, p = units[u]
        r0 = bi * R + ch * T
        return t[r0:r0 + T, p * LANES:(p + 1) * LANES]

    nu = range(len(units))
    heads = [(u, hh) for u in nu for hh in range(2)]
    zero = jnp.zeros((T, LANES), F32)

    lw_hi, lw_lo = _split2(lw)
    rt, vb, atb, dend_t, dt_row, sc = [], [], [], [], [], []
    for u in nu:
        lwu = tile(lw, u)
        clu = _dot(tri_incl_bf, tile(lw_hi, u)) + _dot(tri_incl_bf, tile(lw_lo, u))
        cl_last = clu[T - 1:T, :]
        dinv = jnp.exp(-clu)
        dend = jnp.exp(cl_last - clu)
        at = tile(a_vec, u) * jnp.exp(clu - lwu)
        bt = tile(b_vec, u) * dinv
        kt = tile(k2, u) * dinv
        rtu = tile(r, u) * jnp.exp(clu)
        rt.append(rtu)
        vb.append(tile(v, u).astype(BF16))
        atb.append(at.astype(BF16))
        dend_t.append(jnp.concatenate([(tile(b_vec, u) * dend).T, (tile(k2, u) * dend).T], axis=1).astype(BF16))
        dt_row.append(jnp.exp(cl_last))
        lhs = jnp.concatenate([jnp.where(head0, at, zero), jnp.where(head0, zero, at),
                               jnp.where(head0, rtu, zero), jnp.where(head0, zero, rtu)], axis=0).astype(BF16)
        rhs = jnp.concatenate([bt, kt], axis=0).astype(BF16)
        sc.append(_dot_nt(lhs, rhs))

    aab, aak, arbk = {}, {}, {}
    for (u, hh) in heads:
        aab[u, hh] = jnp.where(strict, sc[u][hh * T:(hh + 1) * T, 0:T], 0.0)
        aak[u, hh] = jnp.where(strict, sc[u][hh * T:(hh + 1) * T, T:2 * T], 0.0).astype(BF16)
        arbk[u, hh] = jnp.where(jnp.concatenate([incl, incl], axis=1), sc[u][(2 + hh) * T:(3 + hh) * T, :],
                                0.0).astype(BF16)

    tinv = {hd: jnp.where(eye, 1.0, jnp.where(level_masks[0], aab[hd], 0.0)).astype(BF16) for hd in heads}
    for lm in level_masks[1:]:
        half = {hd: _dot(tinv[hd], jnp.where(lm, aab[hd], 0.0).astype(BF16)).astype(BF16) for hd in heads}
        tinv = {hd: (tinv[hd].astype(F32) + _dot(half[hd], tinv[hd])).astype(BF16) for hd in heads}

    aakv = {hd: _dot(aak[hd], vb[hd[0]]).astype(BF16) for hd in heads}
    ap_h = {hd: _dot(tinv[hd], atb[hd[0]]) for hd in heads}
    w2_h = {hd: _dot(tinv[hd], aakv[hd]) for hd in heads}
    apn = [jnp.where(head0, ap_h[u, 0], ap_h[u, 1]).astype(BF16) for u in nu]
    w2v = [jnp.concatenate([jnp.where(head0, w2_h[u, 0], w2_h[u, 1]).astype(BF16), vb[u]], axis=0) for u in nu]
    qp_h = {hd: _dot(arbk[hd][:, 0:T], apn[hd[0]]) for hd in heads}
    yl_h = {hd: _dot(arbk[hd], w2v[hd[0]]) for hd in heads}

    ys = []
    for u in nu:
        qp = rt[u] + jnp.where(head0, qp_h[u, 0], qp_h[u, 1])
        yl = jnp.where(head0, yl_h[u, 0], yl_h[u, 1])
        g_mat = jnp.where(same_head, _dot(dend_t[u][:, 0:T], apn[u]), 0.0) + jnp.where(eye_p, dt_row[u], 0.0)
        c_mat = jnp.where(same_head, _dot(dend_t[u], w2v[u]), 0.0)
        state = units[u][0] * n_pairs + units[u][2]
        m0 = m_ref[state].astype(BF16)
        ys.append(_dot(qp.astype(BF16), m0) + yl)
        m_ref[state] = _dot(g_mat.astype(BF16), m0) + c_mat

    y = jnp.concatenate([jnp.concatenate(ys[g * n_pairs:(g + 1) * n_pairs], axis=1) for g in range(bb * nck)],
                        axis=0)
    inv_n = 1.0 / HEAD_DIM
    mean = segsum(y) * inv_n
    yc = y - mean
    var = segsum(yc * yc) * inv_n
    yn = yc * lax.rsqrt(var + GN_EPS) * lng_ref[...] + lnb_ref[...]
    o_ref[...] = ((yn + bonus) * g).astype(BF16).reshape(bb, R, d_rw)


def _wkv(feat_rw, mu, w0, wlw, a0, wla, wlg, k_k, k_a, r_k, lnx_g, lnx_b, bd, tri, d_rw, bb, nck):
    B, Lp, c_rw = feat_rw.shape
    T = WKV_CHUNK
    vec = _const_spec((1, d_rw))
    return pl.pallas_call(
        functools.partial(_wkv_kernel, d_rw=d_rw, bb=bb, nck=nck),
        grid=(B // bb, Lp // (nck * T)),
        in_specs=[
            pl.BlockSpec((bb, nck * T, c_rw), lambda b, c: (b, c, 0)),
            _const_spec((1, c_rw)),
            vec, _const_spec((LANES, d_rw)), vec, _const_spec((LANES, d_rw)), _const_spec((LORA_G, d_rw)),
            vec, vec, vec, vec, vec,
            _const_spec((LANES, LANES)), _const_spec((T, T)),
        ],
        out_specs=pl.BlockSpec((bb, nck * T, d_rw), lambda b, c: (b, c, 0)),
        out_shape=jax.ShapeDtypeStruct((B, Lp, d_rw), BF16),
        scratch_shapes=[pltpu.VMEM((bb * (d_rw // LANES), LANES, LANES), F32),
                        pltpu.VMEM((bb, 1, c_rw), F32)],
        compiler_params=_params(("parallel", "arbitrary")),
        name="wkv",
    )(feat_rw, mu, w0, wlw, a0, wla, wlg, k_k, k_a, r_k, lnx_g, lnx_b, bd, tri)


def _inprep_kernel(h_ref, g_ref, wrw_ref, wg_ref, bg_ref, wfx_ref, qg_ref, kg_ref, bf_ref, sq_ref, sk_ref,
                   cq_ref, ck_ref, tri_ref, rw_ref, gate_ref, q_ref, kt_ref, v_ref, f2_ref, carry_ref, *, n_heads):
    i = pl.program_id(1)

    @pl.when(i == 0)
    def _():
        carry_ref[...] = jnp.zeros_like(carry_ref)

    d_fx = n_heads * HEAD_DIM
    u = _rmsnorm_bf16(h_ref[...], g_ref[...])
    rw_ref[...] = _dot(u, wrw_ref[...])
    gate_ref[...] = _sigmoid(_dot(u, wg_ref[...]) + bg_ref[...]).astype(BF16)
    x = _dot(u, wfx_ref[...])
    tm = x.shape[0]
    fl = x[:, 3 * d_fx:3 * d_fx + LANES]
    z = fl + bf_ref[...]
    log_f = jnp.minimum(z, 0.0) - jnp.log(1.0 + jnp.exp(-jnp.abs(z)))
    lane = lax.broadcasted_iota(jnp.int32, (tm, LANES), 1)
    log_f = jnp.where(lane < n_heads, log_f, 0.0)
    fcum = carry_ref[...] + _dot_exact_lhs(tri_ref[...], log_f)
    carry_ref[...] = fcum[tm - 1:tm, :]
    f2 = fcum * LOG2E
    f2_ref[...] = f2
    hi, mid, lo = _split3(f2)
    parts = (hi.astype(F32) + pltpu.roll(mid.astype(F32), n_heads, 1)
             + pltpu.roll(lo.astype(F32), 2 * n_heads, 1)).astype(BF16)
    extra_q = _dot(parts, sq_ref[...]) + cq_ref[...]
    extra_k = _dot(parts, sk_ref[...]) + ck_ref[...]

    low = lane < HEAD_DIM
    inv_n = 1.0 / HEAD_DIM

    def pair_norm(p, gain):
        p2 = p * p
        ms_lo = jnp.sum(jnp.where(low, p2, 0.0), axis=-1, keepdims=True) * inv_n
        ms_hi = jnp.sum(jnp.where(low, 0.0, p2), axis=-1, keepdims=True) * inv_n
        return p * jnp.where(low, lax.rsqrt(ms_lo + NORM_EPS), lax.rsqrt(ms_hi + NORM_EPS)) * gain

    for kp in range(n_heads // 2):
        ps = slice(kp * LANES, (kp + 1) * LANES)
        te = slice(2 * kp * LANES, (2 * kp + 1) * LANES)
        to = slice((2 * kp + 1) * LANES, (2 * kp + 2) * LANES)
        qn = pair_norm(x[:, ps], qg_ref[:, ps])
        kn = pair_norm(x[:, d_fx + kp * LANES:d_fx + (kp + 1) * LANES], kg_ref[:, ps])
        vp = x[:, 2 * d_fx + kp * LANES:2 * d_fx + (kp + 1) * LANES]
        q_ref[:, te] = jnp.where(low, qn, extra_q[:, te]).astype(BF16)
        q_ref[:, to] = jnp.where(low, extra_q[:, to], qn).astype(BF16)
        kt_ref[te, :] = jnp.where(low, kn, extra_k[:, te]).T.astype(BF16)
        kt_ref[to, :] = jnp.where(low, extra_k[:, to], kn).T.astype(BF16)
        v_ref[:, te] = jnp.where(low, vp, 1.0).astype(BF16)
        v_ref[:, to] = jnp.where(low, 1.0, vp).astype(BF16)


def _inprep(h3, g, wrw, wg, bg, wfx, qg, kg, bf, sq, sk, cq, ck, tri, n_heads, tm):
    B, Lp, d = h3.shape
    n_rw, n_g, n_fx = wrw.shape[1], wg.shape[1], wfx.shape[1]
    d_fx = n_heads * HEAD_DIM
    hw = n_heads * LANES
    return pl.pallas_call(
        functools.partial(_inprep_kernel, n_heads=n_heads),
        grid=(B, Lp // tm),
        in_specs=[
            pl.BlockSpec((None, tm, d), lambda b, i: (b, i, 0)),
            _const_spec((1, d)), _const_spec((d, n_rw)), _const_spec((d, n_g)), _const_spec((1, n_g)),
            _const_spec((d, n_fx)),
            _const_spec((1, d_fx)), _const_spec((1, d_fx)), _const_spec((1, LANES)),
            _const_spec((LANES, hw)), _const_spec((LANES, hw)),
            _const_spec((1, hw)), _const_spec((1, hw)), _const_spec((tm, tm)),
        ],
        out_specs=[
            pl.BlockSpec((None, tm, n_rw), lambda b, i: (b, i, 0)),
            pl.BlockSpec((None, tm, n_g), lambda b, i: (b, i, 0)),
            pl.BlockSpec((None, tm, hw), lambda b, i: (b, i, 0)),
            pl.BlockSpec((None, hw, tm), lambda b, i: (b, 0, i)),
            pl.BlockSpec((None, tm, hw), lambda b, i: (b, i, 0)),
            pl.BlockSpec((None, tm, LANES), lambda b, i: (b, i, 0)),
        ],
        out_shape=[
            jax.ShapeDtypeStruct((B, Lp, n_rw), F32),
            jax.ShapeDtypeStruct((B, Lp, n_g), BF16),
            jax.ShapeDtypeStruct((B, Lp, hw), BF16),
            jax.ShapeDtypeStruct((B, hw, Lp), BF16),
            jax.ShapeDtypeStruct((B, Lp, hw), BF16),
            jax.ShapeDtypeStruct((B, Lp, LANES), F32),
        ],
        scratch_shapes=[pltpu.VMEM((1, LANES), F32)],
        compiler_params=_params(("parallel", "arbitrary")),
        name="inprep",
    )(h3, g, wrw, wg, bg, wfx, qg, kg, bf, sq, sk, cq, ck, tri)


def _attn_kernel(jlo_ref, q_ref, kt_ref, v_ref, o_ref, acc_scr, m_scr, *, blk, nh, online):
    b, g, i = pl.program_id(0), pl.program_id(1), pl.program_id(2)
    lo = jlo_ref[(b * pl.num_programs(1) + g) * pl.num_programs(2) + i]
    ri = lax.broadcasted_iota(jnp.int32, (blk, blk), 0)
    ci = lax.broadcasted_iota(jnp.int32, (blk, blk), 1)
    causal = ci <= ri
    lane = lax.broadcasted_iota(jnp.int32, (blk, LANES), 1)
    hs = [slice(h * LANES, (h + 1) * LANES) for h in range(nh)]

    def block(j, diagonal):
        off = pl.multiple_of(j * blk, blk)
        s = [_dot(q_ref[:, hs[h]], kt_ref[hs[h], pl.ds(off, blk)]) for h in range(nh)]
        probs, alphas = [], []
        for h in range(nh):
            sm = jnp.where(causal, s[h], NEG_BIG) if diagonal else s[h]
            if online:
                row_max = jnp.max(sm, axis=-1, keepdims=True)
                if diagonal:
                    m_new = row_max
                else:
                    m_old = m_scr[h]
                    m_new = jnp.maximum(m_old, row_max)
                    alphas.append(jnp.exp2(m_old - m_new))
                m_scr[h] = m_new
                sm = sm - m_new
            probs.append(jnp.exp2(sm).astype(BF16))
        for h in range(nh):
            pv = _dot(probs[h], v_ref[pl.ds(off, blk), hs[h]])
            if diagonal:
                acc_scr[h] = pv
            elif online:
                acc_scr[h] = alphas[h] * acc_scr[h] + pv
            else:
                acc_scr[h] += pv

    block(i, True)

    def body(n, carry):
        block(i - 1 - n, False)
        return carry

    lax.fori_loop(0, i - lo, body, 0)
    pairs = []
    for k in range(nh // 2):
        even, odd = acc_scr[2 * k], acc_scr[2 * k + 1]
        num = jnp.where(lane < HEAD_DIM, even, odd)
        den = pltpu.roll(jnp.where(lane < HEAD_DIM, odd, even), HEAD_DIM, 1)
        pairs.append(num / den)
    o_ref[...] = jnp.concatenate(pairs, axis=1).astype(BF16)


def _attn(jlo, qp, ktp, vp, blk, nh, online):
    B, Lp, hw = qp.shape
    n_groups = hw // (nh * LANES)
    gw = nh * LANES
    grid_spec = pltpu.PrefetchScalarGridSpec(
        num_scalar_prefetch=1,
        grid=(B, n_groups, Lp // blk),
        in_specs=[
            pl.BlockSpec((None, blk, gw), lambda b, g, i, jlo_ref: (b, i, g)),
            pl.BlockSpec((None, gw, Lp), lambda b, g, i, jlo_ref: (b, g, 0), pipeline_mode=pl.Buffered(1)),
            pl.BlockSpec((None, Lp, gw), lambda b, g, i, jlo_ref: (b, 0, g), pipeline_mode=pl.Buffered(1)),
        ],
        out_specs=pl.BlockSpec((None, blk, gw // 2), lambda b, g, i, jlo_ref: (b, i, g)),
        scratch_shapes=[pltpu.VMEM((nh, blk, LANES), F32), pltpu.VMEM((nh, blk, 1), F32)],
    )
    return pl.pallas_call(
        functools.partial(_attn_kernel, blk=blk, nh=nh, online=online),
        grid_spec=grid_spec,
        out_shape=jax.ShapeDtypeStruct((B, Lp, hw // 2), BF16),
        compiler_params=_params(("parallel", "parallel", "arbitrary")),
        name="attn_online" if online else "attn",
    )(jlo, qp, ktp, vp)


def _first_live_block(f2, span, blk, n_heads, group):
    B, Lp, _ = f2.shape
    nq = Lp // blk
    f_first = f2[:, 0::blk, :n_heads]
    f_last = f2[:, blk - 1::blk, :n_heads]
    gap = f_first[:, :, None, :] - f_last[:, None, :, :]
    jj = jnp.arange(nq)
    live = (gap + span >= -SKIP_LOG2) | (jj[None, None, :, None] >= jj[None, :, None, None])
    jlo = jnp.argmax(live, axis=2).astype(jnp.int32)
    jlo = jnp.min(jlo.reshape(B, nq, n_heads // group, group), axis=3)
    return jnp.transpose(jlo, (0, 2, 1)).reshape(-1)


def _outmlp_kernel(h_ref, yrw_ref, yfx_ref, gate_ref, wrw_ref, wfx_ref, wo_ref, gm_ref, wup_ref,
                   wdn_ref, o_ref, *, ff_chunk):
    d = h_ref.shape[1]
    gates = gate_ref[...].astype(F32)
    merged = gates[:, :d] * _dot(yrw_ref[...], wrw_ref[...]) + gates[:, d:] * _dot(yfx_ref[...], wfx_ref[...])
    h1 = h_ref[...] + _dot(merged.astype(BF16), wo_ref[...])
    ms = jnp.mean(h1 * h1, axis=-1, keepdims=True)
    z = (h1 * lax.rsqrt(ms + NORM_EPS) * gm_ref[...]).astype(BF16)
    acc = h1
    for c0 in range(0, wup_ref.shape[1], ff_chunk):
        u = jnp.maximum(_dot(z, wup_ref[:, c0:c0 + ff_chunk]), 0.0)
        acc = acc + _dot((u * u).astype(BF16), wdn_ref[c0:c0 + ff_chunk, :])
    o_ref[...] = acc


def _outmlp(h2, yrw, yfx, gates, wrw, wfx, wo, gm, wup, wdn, tm):
    rows, d = h2.shape
    d_rw, d_fx, d_ff = wrw.shape[0], wfx.shape[0], wup.shape[1]
    return pl.pallas_call(
        functools.partial(_outmlp_kernel, ff_chunk=min(d_ff, 1024)),
        grid=(rows // tm,),
        in_specs=[
            pl.BlockSpec((tm, d), lambda i: (i, 0)),
            pl.BlockSpec((tm, d_rw), lambda i: (i, 0)),
            pl.BlockSpec((tm, d_fx), lambda i: (i, 0)),
            pl.BlockSpec((tm, 2 * d), lambda i: (i, 0)),
            _const_spec((d_rw, d)), _const_spec((d_fx, d)), _const_spec((d, d)),
            _const_spec((1, d)), _const_spec((d, d_ff)), _const_spec((d_ff, d)),
        ],
        out_specs=pl.BlockSpec((tm, d), lambda i: (i, 0)),
        out_shape=jax.ShapeDtypeStruct((rows, d), F32),
        compiler_params=_params(("parallel",)),
        name="outmlp",
    )(h2, yrw, yfx, gates, wrw, wfx, wo, gm, wup, wdn)


def _largest_divisor(n, candidates):
    for c in candidates:
        if n % c == 0:
            return c
    raise ValueError(f"no tile in {candidates} divides {n}")


def _attn_constants(n_heads, shift):
    hw = n_heads * LANES
    sq = np.zeros((LANES, hw), np.float32)
    sk = np.zeros((LANES, hw), np.float32)
    cq = np.zeros((1, hw), np.float32)
    cq_shift = np.zeros((1, hw), np.float32)
    ck = np.zeros((1, hw), np.float32)
    for h in range(n_heads):
        base = h * LANES + (HEAD_DIM if h % 2 == 0 else 0)
        for part in range(3):
            sq[part * n_heads + h, base + part] = 1.0
            sk[part * n_heads + h, base + 3 + part] = -1.0
            cq[0, base + 3 + part] = 1.0
            ck[0, base + part] = 1.0
        cq_shift[0, base + 6] = -1.0
        ck[0, base + 6] = 1.0
    return (jnp.asarray(sq, BF16), jnp.asarray(sk, BF16), jnp.asarray(cq) + shift * jnp.asarray(cq_shift),
            jnp.asarray(ck))


def kernel(x, meta, norm_mix, w_in, b_gate, b_f, tm_mu, w0, w_lora_up, a0, a_lora_up, g_lora_up, k_k, k_a,
           r_k, lnx_g, lnx_b, q_gain, k_gain, w_out_rw, w_out_fx, w_o, norm_mlp, w_up, w_down):
    B, S, D = x.shape
    depth = w_in.shape[0]
    d_rw = w0.shape[1]
    d_fx = w_out_fx.shape[1]
    h_fx = b_f.shape[1]
    c_rw = 3 * d_rw + LORA_W + LORA_A + LORA_G
    c_fx = 3 * d_fx + h_fx
    assert d_rw % LANES == 0 and h_fx % ATTN_GROUP == 0 and d_fx == h_fx * HEAD_DIM and 3 * h_fx <= LANES
    assert LORA_W + LORA_A == LANES and LORA_G == LANES and HEAD_DIM * 2 == LANES

    L = N_META + S
    Lp = -(-L // SEQ_ALIGN) * SEQ_ALIGN
    rows = B * Lp
    tm_row = _largest_divisor(rows, (512, 256, 128))
    tm_prep = _largest_divisor(Lp, (256, 128))
    blk = SEQ_ALIGN
    wkv_bb = _largest_divisor(B, (WKV_BATCH_ROWS, 1))
    wkv_nck = _largest_divisor(Lp // WKV_CHUNK, (WKV_CHUNKS_PER_STEP, 1))

    meta_b = jnp.broadcast_to(meta.astype(x.dtype)[None], (B, N_META, D))
    h = jnp.concatenate([meta_b, x, jnp.zeros((B, Lp - L, D), x.dtype)], axis=1).reshape(rows, D)

    tri_wkv = jnp.asarray(np.tri(WKV_CHUNK), BF16)
    tri_prep = jnp.asarray(np.tri(tm_prep), BF16)
    seg = jnp.arange(LANES) // HEAD_DIM
    bd = (seg[:, None] == seg[None, :]).astype(BF16)
    zeros_lora = jnp.zeros((LORA_W, d_rw), F32)
    scale = HEAD_DIM ** -0.5
    row = lambda t: t.reshape(1, -1)

    for l in range(depth):
        wl = w_in[l]
        w_rw = wl[:, :c_rw].astype(BF16)
        w_fx = jnp.pad(wl[:, c_rw:c_rw + c_fx], ((0, 0), (0, LANES - h_fx))).astype(BF16)
        w_g = wl[:, c_rw + c_fx:].astype(BF16)

        qg = jnp.tile(q_gain[l] * (scale * LOG2E), h_fx)[None, :]
        kg = jnp.tile(k_gain[l], h_fx)[None, :]
        bf = jnp.pad(b_f[l], (0, LANES - h_fx))[None, :]
        cmax = 1.02 * HEAD_DIM * jnp.max(jnp.abs(qg)) * jnp.max(jnp.abs(kg))
        shift = (1.01 * cmax).astype(BF16).astype(F32)
        sq, sk, cq, ck = _attn_constants(h_fx, shift)
        feat_rw, gates, qp, ktp, vp, f2 = _inprep(h.reshape(B, Lp, D), row(norm_mix[l]), w_rw, w_g, row(b_gate[l]), w_fx,
                                                  qg, kg, bf, sq, sk, cq, ck, tri_prep, h_fx, tm_prep)
        y_rw = _wkv(
            feat_rw, row(tm_mu[l]), row(w0[l]),
            jnp.concatenate([w_lora_up[l], zeros_lora], axis=0).astype(BF16), row(a0[l]),
            jnp.concatenate([zeros_lora, a_lora_up[l]], axis=0).astype(BF16), g_lora_up[l].astype(BF16),
            row(k_k[l]), row(k_a[l]), row(r_k[l]), row(lnx_g[l]), row(lnx_b[l]), bd, tri_wkv, d_rw, wkv_bb, wkv_nck)

        y_fx = lax.cond(
            cmax <= FIXED_SHIFT_MAX_LOG2,
            lambda ops: _attn(_first_live_block(f2, 0.0, blk, h_fx, ATTN_GROUP), *ops, blk, ATTN_GROUP, False),
            lambda ops: _attn(_first_live_block(f2, 2.0 * cmax, blk, h_fx, ATTN_GROUP), *ops, blk, ATTN_GROUP, True),
            (qp, ktp, vp))

        h = _outmlp(h, y_rw.reshape(rows, d_rw), y_fx.reshape(rows, d_fx), gates.reshape(rows, 2 * D),
                    w_out_rw[l].astype(BF16), w_out_fx[l].astype(BF16), w_o[l].astype(BF16),
                    row(norm_mlp[l]), w_up[l].astype(BF16), w_down[l].astype(BF16), tm_row)

    return h.reshape(B, Lp, D)[:, N_META:N_META + S]
```

```python
import functools

import jax
import jax.numpy as jnp
import numpy as np
from jax import lax
from jax.experimental import pallas as pl
from jax.experimental.pallas import tpu as pltpu

F32 = jnp.float32
BF16 = jnp.bfloat16

N_META = 16
HEAD_DIM = 64
LORA_W = 64
LORA_A = 64
LORA_G = 128
NORM_EPS = 1e-6
GN_EPS = 64e-5
DECAY_OFFSET = 0.5
LOG2E = 1.4426950408889634

LANES = 128
SEQ_ALIGN = 256
SKIP_LOG2 = 160.0
WKV_CHUNK = 128
WKV_CHUNKS_PER_STEP = 3
WKV_BATCH_ROWS = 2
ATTN_GROUP = 8
FIXED_SHIFT_MAX_LOG2 = 50.0
VMEM_LIMIT_BYTES = 56 * 1024 * 1024
NEG_BIG = -1e30


def _dot(a, b):
    return jnp.dot(a, b, preferred_element_type=F32)


def _dot_nt(a, b):
    return lax.dot_general(a, b, (((1,), (1,)), ((), ())), preferred_element_type=F32)


def _split2(x):
    hi = x.astype(BF16)
    lo = (x - hi.astype(F32)).astype(BF16)
    return hi, lo


def _split3(x):
    hi = x.astype(BF16)
    r1 = x - hi.astype(F32)
    mid = r1.astype(BF16)
    lo = (r1 - mid.astype(F32)).astype(BF16)
    return hi, mid, lo


def _dot_exact_lhs(m_bf16, x):
    hi, mid, lo = _split3(x)
    return _dot(m_bf16, hi) + _dot(m_bf16, mid) + _dot(m_bf16, lo)


def _sigmoid(x):
    return 1.0 / (1.0 + jnp.exp(-x))


def _softplus(x):
    return jnp.maximum(x, 0.0) + jnp.log(1.0 + jnp.exp(-jnp.abs(x)))


def _const_spec(shape):
    nd = len(shape)
    return pl.BlockSpec(shape, lambda *_: (0,) * nd, pipeline_mode=pl.Buffered(1))


def _params(sem):
    return pltpu.CompilerParams(dimension_semantics=sem, vmem_limit_bytes=VMEM_LIMIT_BYTES)


def _rmsnorm_bf16(x, g):
    ms = jnp.mean(x * x, axis=-1, keepdims=True)
    return (x * lax.rsqrt(ms + NORM_EPS) * g).astype(BF16)


def _wkv_kernel(x_ref, mu_ref, w0_ref, wlw_ref, a0_ref, wla_ref, wlg_ref, kk_ref, ka_ref, rk_ref,
                lng_ref, lnb_ref, bd_ref, tri_ref, o_ref, m_ref, prev_ref, *, d_rw, bb, nck):
    T = WKV_CHUNK
    c = pl.program_id(1)
    n_pairs = d_rw // LANES
    c_rw = x_ref.shape[2]

    @pl.when(c == 0)
    def _():
        m_ref[...] = jnp.zeros_like(m_ref)
        prev_ref[...] = jnp.zeros_like(prev_ref)

    R = nck * T
    x = x_ref[...].reshape(bb * R, c_rw)
    row = lax.broadcasted_iota(jnp.int32, x.shape, 0)
    shifted = pltpu.roll(x, 1, 0)
    for bi in range(bb):
        shifted = jnp.where(row == bi * R, prev_ref[bi], shifted)
        prev_ref[bi] = x[(bi + 1) * R - 1:(bi + 1) * R, :]
    f = x + mu_ref[...] * (shifted - x)

    r = f[:, 0:d_rw]
    k = f[:, d_rw:2 * d_rw]
    v = f[:, 2 * d_rw:3 * d_rw]
    wa_d = f[:, 3 * d_rw:3 * d_rw + LORA_W + LORA_A]
    g_d = f[:, 3 * d_rw + LORA_W + LORA_A:]

    bd = bd_ref[...]

    def segsum(t):
        stacked = jnp.concatenate([t[:, p * LANES:(p + 1) * LANES] for p in range(n_pairs)], axis=0)
        out = _dot(stacked.astype(BF16), bd)
        rows = t.shape[0]
        return jnp.concatenate([out[p * rows:(p + 1) * rows] for p in range(n_pairs)], axis=1)

    w_log = -_softplus(-(w0_ref[...] + _dot(jnp.tanh(wa_d).astype(BF16), wlw_ref[...]))) - DECAY_OFFSET
    lw = -jnp.exp(w_log)
    a_lr = _sigmoid(a0_ref[...] + _dot(wa_d.astype(BF16), wla_ref[...]))
    g = _dot(_sigmoid(g_d).astype(BF16), wlg_ref[...])
    kk = k * kk_ref[...]
    k2 = k * (1.0 + (a_lr - 1.0) * ka_ref[...])
    sums = segsum(jnp.concatenate([kk * kk, r * k2 * rk_ref[...]], axis=0))
    kk = kk * lax.rsqrt(jnp.maximum(sums[:bb * R], 1e-24))
    bonus = sums[bb * R:] * v
    a_vec = -kk
    b_vec = kk * a_lr

    ri = lax.broadcasted_iota(jnp.int32, (T, T), 0)
    ci = lax.broadcasted_iota(jnp.int32, (T, T), 1)
    strict = ci < ri
    incl = ci <= ri
    eye = ci == ri
    tri_incl_bf = tri_ref[...]
    rc_xor = jnp.bitwise_xor(ri, ci)
    level_masks = []
    n = 1
    while n < T:
        level_masks.append(strict & (rc_xor >= n) & (rc_xor < 2 * n))
        n *= 2
    lane = lax.broadcasted_iota(jnp.int32, (T, LANES), 1)
    head0 = lane < HEAD_DIM
    pr = lax.broadcasted_iota(jnp.int32, (LANES, LANES), 0)
    pc = lax.broadcasted_iota(jnp.int32, (LANES, LANES), 1)
    same_head = (pr < HEAD_DIM) == (pc < HEAD_DIM)
    eye_p = pr == pc

    units = [(bi, ch, p) for bi in range(bb) for ch in range(nck) for p in range(n_pairs)]

    def tile(t, u):
        bi, ch, p = units[u]
        r0 = bi * R + ch * T
        return t[r0:r0 + T, p * LANES:(p + 1) * LANES]

    nu = range(len(units))
    heads = [(u, hh) for u in nu for hh in range(2)]
    zero = jnp.zeros((T, LANES), F32)

    lw_hi, lw_lo = _split2(lw)
    rt, vb, atb, dend_t, dt_row, sc = [], [], [], [], [], []
    for u in nu:
        lwu = tile(lw, u)
        clu = _dot(tri_incl_bf, tile(lw_hi, u)) + _dot(tri_incl_bf, tile(lw_lo, u))
        cl_last = clu[T - 1:T, :]
        dinv = jnp.exp(-clu)
        dend = jnp.exp(cl_last - clu)
        at = tile(a_vec, u) * jnp.exp(clu - lwu)
        bt = tile(b_vec, u) * dinv
        kt = tile(k2, u) * dinv
        rtu = tile(r, u) * jnp.exp(clu)
        rt.append(rtu)
        vb.append(tile(v, u).astype(BF16))
        atb.append(at.astype(BF16))
        dend_t.append(jnp.concatenate([(tile(b_vec, u) * dend).T, (tile(k2, u) * dend).T], axis=1).astype(BF16))
        dt_row.append(jnp.exp(cl_last))
        lhs = jnp.concatenate([jnp.where(head0, at, zero), jnp.where(head0, zero, at),
                               jnp.where(head0, rtu, zero), jnp.where(head0, zero, rtu)], axis=0).astype(BF16)
        rhs = jnp.concatenate([bt, kt], axis=0).astype(BF16)
        sc.append(_dot_nt(lhs, rhs))

    aab, aak, arbk = {}, {}, {}
    for (u, hh) in heads:
        aab[u, hh] = jnp.where(strict, sc[u][hh * T:(hh + 1) * T, 0:T], 0.0)
        aak[u, hh] = jnp.where(strict, sc[u][hh * T:(hh + 1) * T, T:2 * T], 0.0).astype(BF16)
        arbk[u, hh] = jnp.where(jnp.concatenate([incl, incl], axis=1), sc[u][(2 + hh) * T:(3 + hh) * T, :],
                                0.0).astype(BF16)

    tinv = {hd: jnp.where(eye, 1.0, jnp.where(level_masks[0], aab[hd], 0.0)).astype(BF16) for hd in heads}
    for lm in level_masks[1:]:
        half = {hd: _dot(tinv[hd], jnp.where(lm, aab[hd], 0.0).astype(BF16)).astype(BF16) for hd in heads}
        tinv = {hd: (tinv[hd].astype(F32) + _dot(half[hd], tinv[hd])).astype(BF16) for hd in heads}

    aakv = {hd: _dot(aak[hd], vb[hd[0]]).astype(BF16) for hd in heads}
    ap_h = {hd: _dot(tinv[hd], atb[hd[0]]) for hd in heads}
    w2_h = {hd: _dot(tinv[hd], aakv[hd]) for hd in heads}
    apn = [jnp.where(head0, ap_h[u, 0], ap_h[u, 1]).astype(BF16) for u in nu]
    w2v = [jnp.concatenate([jnp.where(head0, w2_h[u, 0], w2_h[u, 1]).astype(BF16), vb[u]], axis=0) for u in nu]
    qp_h = {hd: _dot(arbk[hd][:, 0:T], apn[hd[0]]) for hd in heads}
    yl_h = {hd: _dot(arbk[hd], w2v[hd[0]]) for hd in heads}

    ys = []
    for u in nu:
        qp = rt[u] + jnp.where(head0, qp_h[u, 0], qp_h[u, 1])
        yl = jnp.where(head0, yl_h[u, 0], yl_h[u, 1])
        g_mat = jnp.where(same_head, _dot(dend_t[u][:, 0:T], apn[u]), 0.0) + jnp.where(eye_p, dt_row[u], 0.0)
        c_mat = jnp.where(same_head, _dot(dend_t[u], w2v[u]), 0.0)
        state = units[u][0] * n_pairs + units[u][2]
        m0 = m_ref[state].astype(BF16)
        ys.append(_dot(qp.astype(BF16), m0) + yl)
        m_ref[state] = _dot(g_mat.astype(BF16), m0) + c_mat

    y = jnp.concatenate([jnp.concatenate(ys[g * n_pairs:(g + 1) * n_pairs], axis=1) for g in range(bb * nck)],
                        axis=0)
    inv_n = 1.0 / HEAD_DIM
    mean = segsum(y) * inv_n
    yc = y - mean
    var = segsum(yc * yc) * inv_n
    yn = yc * lax.rsqrt(var + GN_EPS) * lng_ref[...] + lnb_ref[...]
    o_ref[...] = ((yn + bonus) * g).astype(BF16).reshape(bb, R, d_rw)


def _wkv(feat_rw, mu, w0, wlw, a0, wla, wlg, k_k, k_a, r_k, lnx_g, lnx_b, bd, tri, d_rw, bb, nck):
    B, Lp, c_rw = feat_rw.shape
    T = WKV_CHUNK
    vec = _const_spec((1, d_rw))
    return pl.pallas_call(
        functools.partial(_wkv_kernel, d_rw=d_rw, bb=bb, nck=nck),
        grid=(B // bb, Lp // (nck * T)),
        in_specs=[
            pl.BlockSpec((bb, nck * T, c_rw), lambda b, c: (b, c, 0)),
            _const_spec((1, c_rw)),
            vec, _const_spec((LANES, d_rw)), vec, _const_spec((LANES, d_rw)), _const_spec((LORA_G, d_rw)),
            vec, vec, vec, vec, vec,
            _const_spec((LANES, LANES)), _const_spec((T, T)),
        ],
        out_specs=pl.BlockSpec((bb, nck * T, d_rw), lambda b, c: (b, c, 0)),
        out_shape=jax.ShapeDtypeStruct((B, Lp, d_rw), BF16),
        scratch_shapes=[pltpu.VMEM((bb * (d_rw // LANES), LANES, LANES), F32),
                        pltpu.VMEM((bb, 1, c_rw), F32)],
        compiler_params=_params(("parallel", "arbitrary")),
        name="wkv",
    )(feat_rw, mu, w0, wlw, a0, wla, wlg, k_k, k_a, r_k, lnx_g, lnx_b, bd, tri)


def _inprep_kernel(h_ref, g_ref, wrw_ref, wg_ref, bg_ref, wfx_ref, qg_ref, kg_ref, bf_ref, sq_ref, sk_ref,
                   cq_ref, ck_ref, tri_ref, rw_ref, gate_ref, q_ref, kt_ref, v_ref, f2_ref, carry_ref, *, n_heads):
    i = pl.program_id(1)

    @pl.when(i == 0)
    def _():
        carry_ref[...] = jnp.zeros_like(carry_ref)

    d_fx = n_heads * HEAD_DIM
    u = _rmsnorm_bf16(h_ref[...], g_ref[...])
    rw_ref[...] = _dot(u, wrw_ref[...])
    gate_ref[...] = _sigmoid(_dot(u, wg_ref[...]) + bg_ref[...]).astype(BF16)
    x = _dot(u, wfx_ref[...])
    tm = x.shape[0]
    fl = x[:, 3 * d_fx:3 * d_fx + LANES]
    z = fl + bf_ref[...]
    log_f = jnp.minimum(z, 0.0) - jnp.log(1.0 + jnp.exp(-jnp.abs(z)))
    lane = lax.broadcasted_iota(jnp.int32, (tm, LANES), 1)
    log_f = jnp.where(lane < n_heads, log_f, 0.0)
    fcum = carry_ref[...] + _dot_exact_lhs(tri_ref[...], log_f)
    carry_ref[...] = fcum[tm - 1:tm, :]
    f2 = fcum * LOG2E
    f2_ref[...] = f2
    hi, mid, lo = _split3(f2)
    parts = (hi.astype(F32) + pltpu.roll(mid.astype(F32), n_heads, 1)
             + pltpu.roll(lo.astype(F32), 2 * n_heads, 1)).astype(BF16)
    extra_q = _dot(parts, sq_ref[...]) + cq_ref[...]
    extra_k = _dot(parts, sk_ref[...]) + ck_ref[...]

    low = lane < HEAD_DIM
    inv_n = 1.0 / HEAD_DIM

    def pair_norm(p, gain):
        p2 = p * p
        ms_lo = jnp.sum(jnp.where(low, p2, 0.0), axis=-1, keepdims=True) * inv_n
        ms_hi = jnp.sum(jnp.where(low, 0.0, p2), axis=-1, keepdims=True) * inv_n
        return p * jnp.where(low, lax.rsqrt(ms_lo + NORM_EPS), lax.rsqrt(ms_hi + NORM_EPS)) * gain

    for kp in range(n_heads // 2):
        ps = slice(kp * LANES, (kp + 1) * LANES)
        te = slice(2 * kp * LANES, (2 * kp + 1) * LANES)
        to = slice((2 * kp + 1) * LANES, (2 * kp + 2) * LANES)
        qn = pair_norm(x[:, ps], qg_ref[:, ps])
        kn = pair_norm(x[:, d_fx + kp * LANES:d_fx + (kp + 1) * LANES], kg_ref[:, ps])
        vp = x[:, 2 * d_fx + kp * LANES:2 * d_fx + (kp + 1) * LANES]
        q_ref[:, te] = jnp.where(low, qn, extra_q[:, te]).astype(BF16)
        q_ref[:, to] = jnp.where(low, extra_q[:, to], qn).astype(BF16)
        kt_ref[te, :] = jnp.where(low, kn, extra_k[:, te]).T.astype(BF16)
        kt_ref[to, :] = jnp.where(low, extra_k[:, to], kn).T.astype(BF16)
        v_ref[:, te] = jnp.where(low, vp, 1.0).astype(BF16)
        v_ref[:, to] = jnp.where(low, 1.0, vp).astype(BF16)


def _inprep(h3, g, wrw, wg, bg, wfx, qg, kg, bf, sq, sk, cq, ck, tri, n_heads, tm):
    B, Lp, d = h3.shape
    n_rw, n_g, n_fx = wrw.shape[1], wg.shape[1], wfx.shape[1]
    d_fx = n_heads * HEAD_DIM
    hw = n_heads * LANES
    return pl.pallas_call(
        functools.partial(_inprep_kernel, n_heads=n_heads),
        grid=(B, Lp // tm),
        in_specs=[
            pl.BlockSpec((None, tm, d), lambda b, i: (b, i, 0)),
            _const_spec((1, d)), _const_spec((d, n_rw)), _const_spec((d, n_g)), _const_spec((1, n_g)),
            _const_spec((d, n_fx)),
            _const_spec((1, d_fx)), _const_spec((1, d_fx)), _const_spec((1, LANES)),
            _const_spec((LANES, hw)), _const_spec((LANES, hw)),
            _const_spec((1, hw)), _const_spec((1, hw)), _const_spec((tm, tm)),
        ],
        out_specs=[
            pl.BlockSpec((None, tm, n_rw), lambda b, i: (b, i, 0)),
            pl.BlockSpec((None, tm, n_g), lambda b, i: (b, i, 0)),
            pl.BlockSpec((None, tm, hw), lambda b, i: (b, i, 0)),
            pl.BlockSpec((None, hw, tm), lambda b, i: (b, 0, i)),
            pl.BlockSpec((None, tm, hw), lambda b, i: (b, i, 0)),
            pl.BlockSpec((None, tm, LANES), lambda b, i: (b, i, 0)),
        ],
        out_shape=[
            jax.ShapeDtypeStruct((B, Lp, n_rw), F32),
            jax.ShapeDtypeStruct((B, Lp, n_g), BF16),
            jax.ShapeDtypeStruct((B, Lp, hw), BF16),
            jax.ShapeDtypeStruct((B, hw, Lp), BF16),
            jax.ShapeDtypeStruct((B, Lp, hw), BF16),
            jax.ShapeDtypeStruct((B, Lp, LANES), F32),
        ],
        scratch_shapes=[pltpu.VMEM((1, LANES), F32)],
        compiler_params=_params(("parallel", "arbitrary")),
        name="inprep",
    )(h3, g, wrw, wg, bg, wfx, qg, kg, bf, sq, sk, cq, ck, tri)


def _attn_kernel(jlo_ref, q_ref, kt_ref, v_ref, o_ref, acc_scr, m_scr, *, blk, nh, online):
    b, g, i = pl.program_id(0), pl.program_id(1), pl.program_id(2)
    lo = jlo_ref[(b * pl.num_programs(1) + g) * pl.num_programs(2) + i]
    ri = lax.broadcasted_iota(jnp.int32, (blk, blk), 0)
    ci = lax.broadcasted_iota(jnp.int32, (blk, blk), 1)
    causal = ci <= ri
    lane = lax.broadcasted_iota(jnp.int32, (blk, LANES), 1)
    hs = [slice(h * LANES, (h + 1) * LANES) for h in range(nh)]

    def block(j, diagonal):
        off = pl.multiple_of(j * blk, blk)
        s = [_dot(q_ref[:, hs[h]], kt_ref[hs[h], pl.ds(off, blk)]) for h in range(nh)]
        probs, alphas = [], []
        for h in range(nh):
            sm = jnp.where(causal, s[h], NEG_BIG) if diagonal else s[h]
            if online:
                row_max = jnp.max(sm, axis=-1, keepdims=True)
                if diagonal:
                    m_new = row_max
                else:
                    m_old = m_scr[h]
                    m_new = jnp.maximum(m_old, row_max)
                    alphas.append(jnp.exp2(m_old - m_new))
                m_scr[h] = m_new
                sm = sm - m_new
            probs.append(jnp.exp2(sm).astype(BF16))
        for h in range(nh):
            pv = _dot(probs[h], v_ref[pl.ds(off, blk), hs[h]])
            if diagonal:
                acc_scr[h] = pv
            elif online:
                acc_scr[h] = alphas[h] * acc_scr[h] + pv
            else:
                acc_scr[h] += pv

    block(i, True)

    def body(n, carry):
        block(i - 1 - n, False)
        return carry

    lax.fori_loop(0, i - lo, body, 0)
    pairs = []
    for k in range(nh // 2):
        even, odd = acc_scr[2 * k], acc_scr[2 * k + 1]
        num = jnp.where(lane < HEAD_DIM, even, odd)
        den = pltpu.roll(jnp.where(lane < HEAD_DIM, odd, even), HEAD_DIM, 1)
        pairs.append(num / den)
    o_ref[...] = jnp.concatenate(pairs, axis=1).astype(BF16)


def _attn(jlo, qp, ktp, vp, blk, nh, online):
    B, Lp, hw = qp.shape
    n_groups = hw // (nh * LANES)
    gw = nh * LANES
    grid_spec = pltpu.PrefetchScalarGridSpec(
        num_scalar_prefetch=1,
        grid=(B, n_groups, Lp // blk),
        in_specs=[
            pl.BlockSpec((None, blk, gw), lambda b, g, i, jlo_ref: (b, i, g)),
            pl.BlockSpec((None, gw, Lp), lambda b, g, i, jlo_ref: (b, g, 0), pipeline_mode=pl.Buffered(1)),
            pl.BlockSpec((None, Lp, gw), lambda b, g, i, jlo_ref: (b, 0, g), pipeline_mode=pl.Buffered(1)),
        ],
        out_specs=pl.BlockSpec((None, blk, gw // 2), lambda b, g, i, jlo_ref: (b, i, g)),
        scratch_shapes=[pltpu.VMEM((nh, blk, LANES), F32), pltpu.VMEM((nh, blk, 1), F32)],
    )
    return pl.pallas_call(
        functools.partial(_attn_kernel, blk=blk, nh=nh, online=online),
        grid_spec=grid_spec,
        out_shape=jax.ShapeDtypeStruct((B, Lp, hw // 2), BF16),
        compiler_params=_params(("parallel", "parallel", "arbitrary")),
        name="attn_online" if online else "attn",
    )(jlo, qp, ktp, vp)


def _first_live_block(f2, span, blk, n_heads, group):
    B, Lp, _ = f2.shape
    nq = Lp // blk
    f_first = f2[:, 0::blk, :n_heads]
    f_last = f2[:, blk - 1::blk, :n_heads]
    gap = f_first[:, :, None, :] - f_last[:, None, :, :]
    jj = jnp.arange(nq)
    live = (gap + span >= -SKIP_LOG2) | (jj[None, None, :, None] >= jj[None, :, None, None])
    jlo = jnp.min(jnp.where(live, jj[None, None, :, None], nq), axis=2).astype(jnp.int32)
    jlo = jnp.min(jlo.reshape(B, nq, n_heads // group, group), axis=3)
    return jnp.transpose(jlo, (0, 2, 1)).reshape(-1)


def _outmlp_kernel(h_ref, yrw_ref, yfx_ref, gate_ref, wrw_ref, wfx_ref, wo_ref, gm_ref, wup_ref,
                   wdn_ref, o_ref, *, ff_chunk):
    d = h_ref.shape[1]
    gates = gate_ref[...].astype(F32)
    merged = gates[:, :d] * _dot(yrw_ref[...], wrw_ref[...]) + gates[:, d:] * _dot(yfx_ref[...], wfx_ref[...])
    h1 = h_ref[...] + _dot(merged.astype(BF16), wo_ref[...])
    ms = jnp.mean(h1 * h1, axis=-1, keepdims=True)
    z = (h1 * lax.rsqrt(ms + NORM_EPS) * gm_ref[...]).astype(BF16)
    acc = h1
    for c0 in range(0, wup_ref.shape[1], ff_chunk):
        u = jnp.maximum(_dot(z, wup_ref[:, c0:c0 + ff_chunk]), 0.0)
        acc = acc + _dot((u * u).astype(BF16), wdn_ref[c0:c0 + ff_chunk, :])
    o_ref[...] = acc


def _outmlp(h2, yrw, yfx, gates, wrw, wfx, wo, gm, wup, wdn, tm):
    rows, d = h2.shape
    d_rw, d_fx, d_ff = wrw.shape[0], wfx.shape[0], wup.shape[1]
    return pl.pallas_call(
        functools.partial(_outmlp_kernel, ff_chunk=min(d_ff, 1024)),
        grid=(rows // tm,),
        in_specs=[
            pl.BlockSpec((tm, d), lambda i: (i, 0)),
            pl.BlockSpec((tm, d_rw), lambda i: (i, 0)),
            pl.BlockSpec((tm, d_fx), lambda i: (i, 0)),
            pl.BlockSpec((tm, 2 * d), lambda i: (i, 0)),
            _const_spec((d_rw, d)), _const_spec((d_fx, d)), _const_spec((d, d)),
            _const_spec((1, d)), _const_spec((d, d_ff)), _const_spec((d_ff, d)),
        ],
        out_specs=pl.BlockSpec((tm, d), lambda i: (i, 0)),
        out_shape=jax.ShapeDtypeStruct((rows, d), F32),
        compiler_params=_params(("parallel",)),
        name="outmlp",
    )(h2, yrw, yfx, gates, wrw, wfx, wo, gm, wup, wdn)


def _largest_divisor(n, candidates):
    for c in candidates:
        if n % c == 0:
            return c
    raise ValueError(f"no tile in {candidates} divides {n}")


def _attn_constants(n_heads, shift):
    hw = n_heads * LANES
    sq = np.zeros((LANES, hw), np.float32)
    sk = np.zeros((LANES, hw), np.float32)
    cq = np.zeros((1, hw), np.float32)
    cq_shift = np.zeros((1, hw), np.float32)
    ck = np.zeros((1, hw), np.float32)
    for h in range(n_heads):
        base = h * LANES + (HEAD_DIM if h % 2 == 0 else 0)
        for part in range(3):
            sq[part * n_heads + h, base + part] = 1.0
            sk[part * n_heads + h, base + 3 + part] = -1.0
            cq[0, base + 3 + part] = 1.0
            ck[0, base + part] = 1.0
        cq_shift[0, base + 6] = -1.0
        ck[0, base + 6] = 1.0
    return (jnp.asarray(sq, BF16), jnp.asarray(sk, BF16), jnp.asarray(cq) + shift * jnp.asarray(cq_shift),
            jnp.asarray(ck))


def kernel(x, meta, norm_mix, w_in, b_gate, b_f, tm_mu, w0, w_lora_up, a0, a_lora_up, g_lora_up, k_k, k_a,
           r_k, lnx_g, lnx_b, q_gain, k_gain, w_out_rw, w_out_fx, w_o, norm_mlp, w_up, w_down):
    B, S, D = x.shape
    depth = w_in.shape[0]
    d_rw = w0.shape[1]
    d_fx = w_out_fx.shape[1]
    h_fx = b_f.shape[1]
    c_rw = 3 * d_rw + LORA_W + LORA_A + LORA_G
    c_fx = 3 * d_fx + h_fx
    assert d_rw % LANES == 0 and h_fx % ATTN_GROUP == 0 and d_fx == h_fx * HEAD_DIM and 3 * h_fx <= LANES
    assert LORA_W + LORA_A == LANES and LORA_G == LANES and HEAD_DIM * 2 == LANES

    L = N_META + S
    Lp = -(-L // SEQ_ALIGN) * SEQ_ALIGN
    rows = B * Lp
    tm_row = _largest_divisor(rows, (512, 256, 128))
    tm_prep = _largest_divisor(Lp, (256, 128))
    blk = SEQ_ALIGN
    wkv_bb = _largest_divisor(B, (WKV_BATCH_ROWS, 1))
    wkv_nck = _largest_divisor(Lp // WKV_CHUNK, (WKV_CHUNKS_PER_STEP, 1))

    meta_b = jnp.broadcast_to(meta.astype(x.dtype)[None], (B, N_META, D))
    h = jnp.concatenate([meta_b, x, jnp.zeros((B, Lp - L, D), x.dtype)], axis=1).reshape(rows, D)

    tri_wkv = jnp.asarray(np.tri(WKV_CHUNK), BF16)
    tri_prep = jnp.asarray(np.tri(tm_prep), BF16)
    seg = jnp.arange(LANES) // HEAD_DIM
    bd = (seg[:, None] == seg[None, :]).astype(BF16)
    zeros_lora = jnp.zeros((LORA_W, d_rw), F32)
    scale = HEAD_DIM ** -0.5
    row = lambda t: t.reshape(1, -1)

    for l in range(depth):
        wl = w_in[l]
        w_rw = wl[:, :c_rw].astype(BF16)
        w_fx = jnp.pad(wl[:, c_rw:c_rw + c_fx], ((0, 0), (0, LANES - h_fx))).astype(BF16)
        w_g = wl[:, c_rw + c_fx:].astype(BF16)

        qg = jnp.tile(q_gain[l] * (scale * LOG2E), h_fx)[None, :]
        kg = jnp.tile(k_gain[l], h_fx)[None, :]
        bf = jnp.pad(b_f[l], (0, LANES - h_fx))[None, :]
        cmax = 1.02 * HEAD_DIM * jnp.max(jnp.abs(qg)) * jnp.max(jnp.abs(kg))
        shift = (1.01 * cmax).astype(BF16).astype(F32)
        sq, sk, cq, ck = _attn_constants(h_fx, shift)
        feat_rw, gates, qp, ktp, vp, f2 = _inprep(h.reshape(B, Lp, D), row(norm_mix[l]), w_rw, w_g, row(b_gate[l]), w_fx,
                                                  qg, kg, bf, sq, sk, cq, ck, tri_prep, h_fx, tm_prep)
        y_rw = _wkv(
            feat_rw, row(tm_mu[l]), row(w0[l]),
            jnp.concatenate([w_lora_up[l], zeros_lora], axis=0).astype(BF16), row(a0[l]),
            jnp.concatenate([zeros_lora, a_lora_up[l]], axis=0).astype(BF16), g_lora_up[l].astype(BF16),
            row(k_k[l]), row(k_a[l]), row(r_k[l]), row(lnx_g[l]), row(lnx_b[l]), bd, tri_wkv, d_rw, wkv_bb, wkv_nck)

        y_fx = lax.cond(
            cmax <= FIXED_SHIFT_MAX_LOG2,
            lambda ops: _attn(_first_live_block(f2, 0.0, blk, h_fx, ATTN_GROUP), *ops, blk, ATTN_GROUP, False),
            lambda ops: _attn(_first_live_block(f2, 2.0 * cmax, blk, h_fx, ATTN_GROUP), *ops, blk, ATTN_GROUP, True),
            (qp, ktp, vp))

        h = _outmlp(h, y_rw.reshape(rows, d_rw), y_fx.reshape(rows, d_fx), gates.reshape(rows, 2 * D),
                    w_out_rw[l].astype(BF16), w_out_fx[l].astype(BF16), w_o[l].astype(BF16),
                    row(norm_mlp[l]), w_up[l].astype(BF16), w_down[l].astype(BF16), tm_row)

    return h.reshape(B, Lp, D)[:, N_META:N_META + S]
```
